```python
import jax, jax.numpy as jnp
from jax import lax
import numpy as np

D_MODEL = 4096
BATCH = 2
SEQ = 8192
DEPTH = 4

GRID_W = 64
CTX_LEN = 256

N_Q_HEADS = 16
N_KV_HEADS = 4
HEAD_DIM = 128
GROUP = N_Q_HEADS // N_KV_HEADS
Q_DIM = N_Q_HEADS * HEAD_DIM
KV_DIM = N_KV_HEADS * HEAD_DIM
WINDOW = 128
BLOCK = 128
ROPE_THETA = 10000.0
NEG_INF = -1e30

LRU_WIDTH = 2048
LRU_BLOCKS = 16
LRU_BLOCK_DIM = LRU_WIDTH // LRU_BLOCKS
CONV_WIDTH = 4
CONV_LEFT = CONV_WIDTH // 2
LRU_C = 8.0

N_EXPERTS = 32
TOP_K = 4
EXPERT_DIM = 256
SWIGLU_LIMIT = 7.0
SWIGLU_ALPHA = 1.702

MOD_RANK = 512
N_MOD = 6
EPS = 1e-6

IN_SIZES = (Q_DIM, KV_DIM, KV_DIM, LRU_WIDTH, LRU_WIDTH, D_MODEL, D_MODEL)
IN_DIM = sum(IN_SIZES)

kernel_name = "hybrid_swa_rglru_moe_diffusion_trunk"


def rmsnorm(x, g):
    xf = x.astype(jnp.float32)
    y = xf * lax.rsqrt(jnp.mean(xf * xf, axis=-1, keepdims=True) + EPS)
    return (y * g.astype(jnp.float32)).astype(x.dtype)


def modulation(cond, w1, w2, b):
    m = (jax.nn.silu(cond) @ w1) @ w2 + b
    return jnp.split(m, N_MOD, axis=-1)


def modulate(h, shift, scale):
    return h * (1 + scale) + shift


def split_proj(z):
    offs, acc = [], 0
    for s in IN_SIZES[:-1]:
        acc += s
        offs.append(acc)
    return jnp.split(z, offs, axis=-1)


def axial_rope_tables(n_tokens):
    rows = n_tokens // GRID_W
    pos_r = jnp.repeat(jnp.arange(rows), GRID_W).astype(jnp.float32)
    pos_c = jnp.tile(jnp.arange(GRID_W), rows).astype(jnp.float32)
    axis_dim = HEAD_DIM // 2
    inv = ROPE_THETA ** (-jnp.arange(0, axis_dim, 2, dtype=jnp.float32) / axis_dim)
    ang_r = pos_r[:, None] * inv[None, :]
    ang_c = pos_c[:, None] * inv[None, :]
    ang = jnp.concatenate([ang_r, ang_r, ang_c, ang_c], axis=-1)
    return jnp.cos(ang), jnp.sin(ang)


def rotate_half_axial(x):
    x1r, x2r, x1c, x2c = jnp.split(x, 4, axis=-1)
    return jnp.concatenate([-x2r, x1r, -x2c, x1c], axis=-1)


def apply_rope(x, cos, sin):
    cos = cos[:, None, :].astype(x.dtype)
    sin = sin[:, None, :].astype(x.dtype)
    return x * cos + rotate_half_axial(x) * sin


def softmax_with_sink(logits, sink):
    sink = jnp.broadcast_to(sink, logits.shape[:-1] + (1,))
    p = jax.nn.softmax(jnp.concatenate([logits, sink], axis=-1), axis=-1)
    return p[..., :-1]


def band(t, nb):
    B = t.shape[0]
    tp = jnp.pad(t, ((0, 0), (BLOCK, BLOCK), (0, 0), (0, 0)))
    tb = tp.reshape(B, nb + 2, BLOCK, t.shape[2], t.shape[3])
    return jnp.concatenate([tb[:, :-2], tb[:, 1:-1], tb[:, 2:]], axis=2)


def windowed_gqa(q, k, v, kc, vc, sinks):
    B, S = q.shape[:2]
    nb = S // BLOCK
    scale = HEAD_DIM ** -0.5
    qb = q.reshape(B, nb, BLOCK, N_KV_HEADS, GROUP, HEAD_DIM)
    kw, vw = band(k, nb), band(v, nb)
    s_loc = jnp.einsum('bnqhgd,bnkhd->bnhgqk', qb, kw).astype(jnp.float32) * scale
    s_ctx = jnp.einsum('bnqhgd,bkhd->bnhgqk', qb, kc).astype(jnp.float32) * scale
    qpos = jnp.arange(S).reshape(nb, BLOCK)
    kpos = jnp.arange(-BLOCK, S + BLOCK).reshape(nb + 2, BLOCK)
    kpos = jnp.concatenate([kpos[:-2], kpos[1:-1], kpos[2:]], axis=1)
    rel = kpos[:, None, :] - qpos[:, :, None]
    valid = (jnp.abs(rel) <= WINDOW) & (kpos[:, None, :] >= 0) & (kpos[:, None, :] < S)
    s_loc = jnp.where(valid[None, :, None, None], s_loc, NEG_INF)
    sink = sinks.astype(jnp.float32).reshape(N_KV_HEADS, GROUP)[None, None, :, :, None, None]
    p = softmax_with_sink(jnp.concatenate([s_loc, s_ctx], axis=-1), sink)
    p_loc = p[..., :3 * BLOCK].astype(v.dtype)
    p_ctx = p[..., 3 * BLOCK:].astype(v.dtype)
    o = (jnp.einsum('bnhgqk,bnkhd->bnqhgd', p_loc, vw)
         + jnp.einsum('bnhgqk,bkhd->bnqhgd', p_ctx, vc))
    return o.reshape(B, S, Q_DIM)


def context_attention(qc, kc, vc, sinks):
    B, C = qc.shape[:2]
    scale = HEAD_DIM ** -0.5
    qg = qc.reshape(B, C, N_KV_HEADS, GROUP, HEAD_DIM)
    s = jnp.einsum('bqhgd,bkhd->bhgqk', qg, kc).astype(jnp.float32) * scale
    sink = sinks.astype(jnp.float32).reshape(N_KV_HEADS, GROUP)[None, :, :, None, None]
    p = softmax_with_sink(s, sink).astype(vc.dtype)
    return jnp.einsum('bhgqk,bkhd->bqhgd', p, vc).reshape(B, C, Q_DIM)


def centred_depthwise_conv(u, w, b):
    S = u.shape[1]
    up = jnp.pad(u, ((0, 0), (CONV_LEFT, CONV_WIDTH - 1 - CONV_LEFT), (0, 0)))
    out = up[:, 0:S] * w[0]
    for i in range(1, CONV_WIDTH):
        out = out + up[:, i:i + S] * w[i]
    return out + b


def block_diag(u, w, b):
    ub = u.reshape(u.shape[:-1] + (LRU_BLOCKS, LRU_BLOCK_DIM))
    return jnp.einsum('...hi,hij->...hj', ub, w).reshape(u.shape) + b


def rglru_coeffs(u, w_r, b_r, w_i, b_i, lam):
    uf = u.astype(jnp.float32)
    r = jax.nn.sigmoid(block_diag(uf, w_r.astype(jnp.float32), b_r.astype(jnp.float32)))
    gi = jax.nn.sigmoid(block_diag(uf, w_i.astype(jnp.float32), b_i.astype(jnp.float32)))
    log_a = -LRU_C * r * jax.nn.softplus(-lam.astype(jnp.float32))
    a = jnp.exp(log_a)
    mult = jnp.sqrt(-jnp.expm1(2 * log_a))
    return a, mult * (gi * uf)


def linear_scan(a, b, h0, reverse):
    if h0 is not None:
        first = -1 if reverse else 0
        b = b.at[:, first].add(a[:, first] * h0)

    def combine(lhs, rhs):
        a1, b1 = lhs
        a2, b2 = rhs
        return a1 * a2, a2 * b1 + b2

    _, h = lax.associative_scan(combine, (a, b), axis=1, reverse=reverse)
    return h


def bidirectional_rglru(u, uc, w_r, b_r, w_i, b_i, lam):
    rec = jnp.zeros(u.shape, jnp.float32)
    rec_c = jnp.zeros(uc.shape, jnp.float32)
    for d in range(2):
        reverse = d == 1
        a_c, b_c = rglru_coeffs(uc, w_r[d], b_r[d], w_i[d], b_i[d], lam[d])
        h_c = linear_scan(a_c, b_c, None, reverse)
        h_final = h_c[:, 0] if reverse else h_c[:, -1]
        a_l, b_l = rglru_coeffs(u, w_r[d], b_r[d], w_i[d], b_i[d], lam[d])
        rec = rec + linear_scan(a_l, b_l, h_final, reverse)
        rec_c = rec_c + h_c
    return rec.astype(u.dtype), rec_c.astype(uc.dtype)


def hybrid_mixer(h, hc, cos, sin, w_in, conv_w, conv_b, lru_wr, lru_br, lru_wi, lru_bi,
                 lru_lam, sinks, w_oa, w_ob, w_out, need_ctx_out):
    B, S, _ = h.shape
    C = hc.shape[1]
    q, k, v, u, gl, ga, gb = split_proj(h @ w_in)
    qc, kc, vc, uc, glc, gac, gbc = split_proj(hc @ w_in)
    q = apply_rope(q.reshape(B, S, N_Q_HEADS, HEAD_DIM), cos, sin)
    k = apply_rope(k.reshape(B, S, N_KV_HEADS, HEAD_DIM), cos, sin)
    v = v.reshape(B, S, N_KV_HEADS, HEAD_DIM)
    kc = kc.reshape(B, C, N_KV_HEADS, HEAD_DIM)
    vc = vc.reshape(B, C, N_KV_HEADS, HEAD_DIM)
    att = windowed_gqa(q, k, v, kc, vc, sinks)
    u = centred_depthwise_conv(u, conv_w, conv_b)
    uc = centred_depthwise_conv(uc, conv_w, conv_b)
    rec, rec_c = bidirectional_rglru(u, uc, lru_wr, lru_br, lru_wi, lru_bi, lru_lam)
    y = (jax.nn.sigmoid(ga) * (att @ w_oa)
         + jax.nn.sigmoid(gb) * ((rec * jax.nn.gelu(gl)) @ w_ob)) @ w_out
    if not need_ctx_out:
        return y, None
    att_c = context_attention(qc.reshape(B, C, N_Q_HEADS, HEAD_DIM), kc, vc, sinks)
    yc = (jax.nn.sigmoid(gac) * (att_c @ w_oa)
          + jax.nn.sigmoid(gbc) * ((rec_c * jax.nn.gelu(glc)) @ w_ob)) @ w_out
    return y, yc


def moe(x, w_router, b_router, w1, b1, w2, b2):
    logits = (x @ w_router + b_router).astype(jnp.float32)
    top_v, top_i = lax.top_k(logits, TOP_K)
    wts = jax.nn.softmax(top_v, axis=-1)
    combine = jnp.sum(wts[..., None] * jax.nn.one_hot(top_i, N_EXPERTS, dtype=wts.dtype), axis=-2)
    combine = combine.astype(x.dtype)
    hdn = jnp.einsum('...d,edf->...ef', x, w1) + b1
    glu, lin = jnp.split(hdn, 2, axis=-1)
    glu = jnp.minimum(glu, SWIGLU_LIMIT)
    lin = jnp.clip(lin, -SWIGLU_LIMIT, SWIGLU_LIMIT)
    act = glu * jax.nn.sigmoid(SWIGLU_ALPHA * glu) * (lin + 1)
    act = act * combine[..., None]
    return jnp.einsum('...ef,efd->...d', act, w2) + combine @ b2


def setup_inputs(seed: int = 0) -> dict:
    key = jax.random.key(seed)
    ks = jax.random.split(key, 32)
    f32 = jnp.float32
    L = DEPTH

    def nrm(k, shape, scale):
        return jax.random.normal(k, shape, f32) * scale

    a_c = jax.random.uniform(ks[16], (L, 2, LRU_WIDTH), f32, 0.9, 0.999)
    a = a_c ** (1.0 / LRU_C)
    return {
        'x': nrm(ks[0], (BATCH, SEQ, D_MODEL), 1.0),
        'c': nrm(ks[1], (BATCH, D_MODEL), 1.0),
        'ctx': nrm(ks[2], (BATCH, CTX_LEN, D_MODEL), 1.0),
        'c_ctx': nrm(ks[3], (D_MODEL,), 1.0),
        'w_mod1': nrm(ks[4], (L, D_MODEL, MOD_RANK), D_MODEL ** -0.5),
        'w_mod2': nrm(ks[5], (L, MOD_RANK, N_MOD * D_MODEL), 0.2 * MOD_RANK ** -0.5),
        'b_mod': nrm(ks[6], (L, N_MOD * D_MODEL), 0.02),
        'g_mix': 1.0 + nrm(ks[7], (L, D_MODEL), 0.02),
        'g_ffn': 1.0 + nrm(ks[8], (L, D_MODEL), 0.02),
        'w_in': nrm(ks[9], (L, D_MODEL, IN_DIM), D_MODEL ** -0.5),
        'conv_w': nrm(ks[10], (L, CONV_WIDTH, LRU_WIDTH), CONV_WIDTH ** -0.5),
        'conv_b': nrm(ks[11], (L, LRU_WIDTH), 0.02),
        'lru_wr': nrm(ks[12], (L, 2, LRU_BLOCKS, LRU_BLOCK_DIM, LRU_BLOCK_DIM), LRU_BLOCK_DIM ** -0.5),
        'lru_br': nrm(ks[13], (L, 2, LRU_WIDTH), 0.02),
        'lru_wi': nrm(ks[14], (L, 2, LRU_BLOCKS, LRU_BLOCK_DIM, LRU_BLOCK_DIM), LRU_BLOCK_DIM ** -0.5),
        'lru_bi': nrm(ks[15], (L, 2, LRU_WIDTH), 0.02),
        'lru_lam': jnp.log(a) - jnp.log1p(-a),
        'sinks': nrm(ks[17], (L, N_Q_HEADS), 0.5),
        'w_oa': nrm(ks[18], (L, Q_DIM, D_MODEL), Q_DIM ** -0.5),
        'w_ob': nrm(ks[19], (L, LRU_WIDTH, D_MODEL), LRU_WIDTH ** -0.5),
        'w_out': nrm(ks[20], (L, D_MODEL, D_MODEL), D_MODEL ** -0.5),
        'w_router': nrm(ks[21], (L, D_MODEL, N_EXPERTS), D_MODEL ** -0.5),
        'b_router': nrm(ks[22], (L, N_EXPERTS), 0.01),
        'w_exp1': nrm(ks[23], (L, N_EXPERTS, D_MODEL, 2 * EXPERT_DIM), D_MODEL ** -0.5),
        'b_exp1': nrm(ks[24], (L, N_EXPERTS, 2 * EXPERT_DIM), 0.02),
        'w_exp2': nrm(ks[25], (L, N_EXPERTS, EXPERT_DIM, D_MODEL), EXPERT_DIM ** -0.5),
        'b_exp2': nrm(ks[26], (L, N_EXPERTS, D_MODEL), 0.02),
        'g_final': 1.0 + nrm(ks[27], (D_MODEL,), 0.02),
    }


def reference(x, c, ctx, c_ctx, w_mod1, w_mod2, b_mod, g_mix, g_ffn, w_in, conv_w, conv_b,
              lru_wr, lru_br, lru_wi, lru_bi, lru_lam, sinks, w_oa, w_ob, w_out,
              w_router, b_router, w_exp1, b_exp1, w_exp2, b_exp2, g_final):
    S = x.shape[1]
    cos, sin = axial_rope_tables(S)
    xc = ctx
    for l in range(DEPTH):
        need_ctx_out = l < DEPTH - 1
        sh1, sc1, gt1, sh2, sc2, gt2 = modulation(c, w_mod1[l], w_mod2[l], b_mod[l][None])
        sh1, sc1, gt1, sh2, sc2, gt2 = [m[:, None, :] for m in (sh1, sc1, gt1, sh2, sc2, gt2)]
        csh1, csc1, cgt1, csh2, csc2, cgt2 = modulation(c_ctx, w_mod1[l], w_mod2[l], b_mod[l])
        h = modulate(rmsnorm(x, g_mix[l]), sh1, sc1)
        hc = modulate(rmsnorm(xc, g_mix[l]), csh1, csc1)
        y, yc = hybrid_mixer(h, hc, cos, sin, w_in[l], conv_w[l], conv_b[l], lru_wr[l], lru_br[l],
                             lru_wi[l], lru_bi[l], lru_lam[l], sinks[l], w_oa[l], w_ob[l], w_out[l],
                             need_ctx_out)
        x = x + gt1 * y
        h2 = modulate(rmsnorm(x, g_ffn[l]), sh2, sc2)
        x = x + gt2 * moe(h2, w_router[l], b_router[l], w_exp1[l], b_exp1[l], w_exp2[l], b_exp2[l])
        if need_ctx_out:
            xc = xc + cgt1 * yc
            hc2 = modulate(rmsnorm(xc, g_ffn[l]), csh2, csc2)
            xc = xc + cgt2 * moe(hc2, w_router[l], b_router[l], w_exp1[l], b_exp1[l], w_exp2[l], b_exp2[l])
    return rmsnorm(x, g_final)
```

```python
import functools
import math

import jax
import jax.numpy as jnp
from jax import lax
from jax.experimental import pallas as pl
from jax.experimental.pallas import tpu as pltpu

GRID_W = 64
N_Q_HEADS = 16
N_KV_HEADS = 4
HEAD_DIM = 128
WINDOW = 128
BLOCK = 128
ROPE_THETA = 10000.0
NEG_INF = -1e30
LRU_BLOCK_DIM = 128
CONV_WIDTH = 4
LRU_C = 8.0
TOP_K = 4
SWIGLU_LIMIT = 7.0
SWIGLU_ALPHA = 1.702
N_MOD = 6
EPS = 1e-6

V7X_VMEM_LIMIT_BYTES = 56 * 1024 * 1024
SUBLANES = 8
LANES = 128

F32 = jnp.float32
BF16 = jnp.bfloat16
HIGHEST = lax.Precision.HIGHEST


def _params(*sem):
    return pltpu.CompilerParams(dimension_semantics=sem, vmem_limit_bytes=V7X_VMEM_LIMIT_BYTES)


def _tile(n, pref):
    if n <= pref:
        return n
    t = pref
    while t >= 8:
        if n % t == 0 and t % 8 == 0:
            return t
        t -= 8
    return n


def _mod_kernel(cond_ref, w1_ref, w2_ref, b_ref, o_ref, t_ref):
    @pl.when(pl.program_id(0) == 0)
    def _():
        cnd = cond_ref[...]
        t_ref[...] = jnp.dot(cnd * jax.nn.sigmoid(cnd), w1_ref[...], precision=HIGHEST,
                             preferred_element_type=F32)

    o_ref[...] = jnp.dot(t_ref[...], w2_ref[...], precision=HIGHEST,
                         preferred_element_type=F32) + b_ref[...]


def _modulation(cond, w1, w2, b):
    g8, d = cond.shape
    mr = w1.shape[1]
    n = w2.shape[1]
    tn = _tile(n, 2048)
    return pl.pallas_call(
        _mod_kernel,
        out_shape=jax.ShapeDtypeStruct((g8, n), F32),
        grid=(n // tn,),
        in_specs=[pl.BlockSpec((g8, d), lambda j: (0, 0)),
                  pl.BlockSpec((d, mr), lambda j: (0, 0)),
                  pl.BlockSpec((mr, tn), lambda j: (0, j)),
                  pl.BlockSpec((1, tn), lambda j: (0, j))],
        out_specs=pl.BlockSpec((g8, tn), lambda j: (0, j)),
        scratch_shapes=[pltpu.VMEM((g8, mr), F32)],
        compiler_params=_params("arbitrary"),
        name="modulation",
    )(cond, w1, w2, b.reshape(1, n))


def _resnorm_kernel(*refs, has_delta, with_router, n_experts):
    refs = list(refs)
    x_ref = refs.pop(0)
    if has_delta:
        d_ref = refs.pop(0)
        gate_ref = refs.pop(0)
    g_ref, sh_ref, sc_ref = refs.pop(0), refs.pop(0), refs.pop(0)
    if with_router:
        wr_ref, br_ref = refs.pop(0), refs.pop(0)
    xo_ref, h_ref = refs.pop(0), refs.pop(0)

    x = x_ref[...]
    if has_delta:
        x = x + gate_ref[...] * d_ref[...].astype(F32)
    xo_ref[...] = x
    ms = jnp.mean(x * x, axis=-1, keepdims=True)
    y = (x * lax.rsqrt(ms + EPS)) * g_ref[...]
    h = y * (1.0 + sc_ref[...]) + sh_ref[...]
    h_ref[...] = h.astype(h_ref.dtype)

    if with_router:
        comb_ref = refs.pop(0)
        logits = jnp.dot(h, wr_ref[...], precision=HIGHEST, preferred_element_type=F32) + br_ref[...]
        lane = lax.broadcasted_iota(jnp.int32, logits.shape, 1)
        work = logits
        comb = jnp.zeros_like(logits)
        denom = jnp.zeros((logits.shape[0], 1), F32)
        top0 = None
        for k in range(TOP_K):
            m = jnp.max(work, axis=-1, keepdims=True)
            idx = jnp.min(jnp.where(work == m, lane, n_experts), axis=-1, keepdims=True)
            onehot = lane == idx
            if k == 0:
                top0 = m
            e = jnp.exp(m - top0)
            comb = comb + jnp.where(onehot, e, 0.0)
            denom = denom + e
            work = jnp.where(onehot, -jnp.inf, work)
        comb_ref[...] = comb / denom


def _resnorm(x, mods, group_of_block, tm, n_rows, g, shift_idx, scale_idx, *, delta=None,
             gate_mods=None, gate_idx=None, router=None):
    r, d = x.shape
    has_delta = delta is not None
    with_router = router is not None

    def row(i):
        return (i, 0)

    def mod_spec(m):
        return pl.BlockSpec((None, 1, d), lambda i: (group_of_block(i) * N_MOD + m, 0, 0))

    in_specs = [pl.BlockSpec((tm, d), row)]
    args = [x]
    if has_delta:
        in_specs += [pl.BlockSpec((tm, d), row), mod_spec(gate_idx)]
        args += [delta, gate_mods]
    in_specs += [pl.BlockSpec((1, d), lambda i: (0, 0)), mod_spec(shift_idx), mod_spec(scale_idx)]
    args += [g.reshape(1, d), mods, mods]
    out_shape = [jax.ShapeDtypeStruct((r, d), F32), jax.ShapeDtypeStruct((r, d), BF16)]
    out_specs = [pl.BlockSpec((tm, d), row), pl.BlockSpec((tm, d), row)]
    n_experts = 0
    if with_router:
        w_router, b_router = router
        n_experts = w_router.shape[1]
        in_specs += [pl.BlockSpec((d, n_experts), lambda i: (0, 0)),
                     pl.BlockSpec((1, n_experts), lambda i: (0, 0))]
        args += [w_router, b_router.reshape(1, n_experts)]
        out_shape.append(jax.ShapeDtypeStruct((r, n_experts), F32))
        out_specs.append(pl.BlockSpec((tm, n_experts), row))
    return pl.pallas_call(
        functools.partial(_resnorm_kernel, has_delta=has_delta, with_router=with_router,
                          n_experts=n_experts),
        out_shape=out_shape,
        grid=(n_rows // tm,),
        in_specs=in_specs,
        out_specs=out_specs,
        input_output_aliases={0: 0},
        compiler_params=_params("arbitrary"),
        name="resnorm",
    )(*args)


def _final_kernel(x_ref, d_ref, gate_ref, g_ref, o_ref):
    x = x_ref[...] + gate_ref[...] * d_ref[...].astype(F32)
    ms = jnp.mean(x * x, axis=-1, keepdims=True)
    o_ref[...] = (x * lax.rsqrt(ms + EPS)) * g_ref[...]


def _final_norm(x, delta, mods, group_of_block, tm, n_rows, g, gate_idx):
    r, d = x.shape
    return pl.pallas_call(
        _final_kernel,
        out_shape=jax.ShapeDtypeStruct((n_rows, d), F32),
        grid=(n_rows // tm,),
        in_specs=[pl.BlockSpec((tm, d), lambda i: (i, 0)),
                  pl.BlockSpec((tm, d), lambda i: (i, 0)),
                  pl.BlockSpec((None, 1, d), lambda i: (group_of_block(i) * N_MOD + gate_idx, 0, 0)),
                  pl.BlockSpec((1, d), lambda i: (0, 0))],
        out_specs=pl.BlockSpec((tm, d), lambda i: (i, 0)),
        compiler_params=_params("arbitrary"),
        name="final_norm",
    )(x, delta, mods, g.reshape(1, d))


def _mm_kernel(a_ref, w_ref, o_ref):
    o_ref[...] = jnp.dot(a_ref[...], w_ref[...], preferred_element_type=F32).astype(o_ref.dtype)


def _mm_rope_kernel(a_ref, w_ref, cos_ref, sin_ref, o_ref):
    acc = jnp.dot(a_ref[...], w_ref[...], preferred_element_type=F32)
    cos = cos_ref[...]
    sin = sin_ref[...]
    lane = lax.broadcasted_iota(jnp.int32, cos.shape, 1)
    first_half = (lane % (HEAD_DIM // 2)) < (HEAD_DIM // 4)
    for c in range(acc.shape[1] // HEAD_DIM):
        xc = acc[:, c * HEAD_DIM:(c + 1) * HEAD_DIM]
        rot = jnp.where(first_half, pltpu.roll(xc, HEAD_DIM - HEAD_DIM // 4, 1),
                        pltpu.roll(xc, HEAD_DIM // 4, 1))
        o_ref[:, c * HEAD_DIM:(c + 1) * HEAD_DIM] = (xc * cos + rot * sin).astype(o_ref.dtype)


def _matmul(a, w, tm, n_rows, tn_pref=512, rope=None):
    r, k = a.shape
    n = w.shape[1]
    tn = _tile(n, tn_pref)
    in_specs = [pl.BlockSpec((tm, k), lambda i, j: (i, 0)),
                pl.BlockSpec((k, tn), lambda i, j: (0, j))]
    args = [a, w]
    kern = _mm_kernel
    if rope is not None:
        in_specs += [pl.BlockSpec((tm, HEAD_DIM), lambda i, j: (i, 0)),
                     pl.BlockSpec((tm, HEAD_DIM), lambda i, j: (i, 0))]
        args += list(rope)
        kern = _mm_rope_kernel
    return pl.pallas_call(
        kern,
        out_shape=jax.ShapeDtypeStruct((r, n), BF16),
        grid=(n_rows // tm, n // tn),
        in_specs=in_specs,
        out_specs=pl.BlockSpec((tm, tn), lambda i, j: (i, j)),
        compiler_params=_params("arbitrary", "arbitrary"),
        name="matmul_rope" if rope is not None else "matmul",
    )(*args)


def _attn_kernel(sink_ref, q_ref, *refs, n_blocks, group, has_local):
    o_ref = refs[-1]
    n = pl.program_id(1)
    h = pl.program_id(2)
    q = q_ref[...]
    qs = jnp.concatenate([q[:, g * HEAD_DIM:(g + 1) * HEAD_DIM] for g in range(group)], axis=0)
    if has_local:
        kp, ko, kn, kc, vp, vo, vn, vc = [ref[...] for ref in refs[:8]]
        kall = jnp.concatenate([kp, ko, kn, kc], axis=0)
        vall = jnp.concatenate([vp, vo, vn, vc], axis=0)
    else:
        kall = refs[0][...]
        vall = refs[1][...]
    s = lax.dot_general(qs, kall, (((1,), (1,)), ((), ())), preferred_element_type=F32)
    s = s * (HEAD_DIM ** -0.5)
    if has_local:
        n_keys = kall.shape[0]
        t = lax.broadcasted_iota(jnp.int32, (BLOCK, n_keys), 0)
        col = lax.broadcasted_iota(jnp.int32, (BLOCK, n_keys), 1)
        rel = col - BLOCK - t
        lo = jnp.where(n > 0, 0, BLOCK)
        hi = jnp.where(n < n_blocks - 1, 3 * BLOCK, 2 * BLOCK)
        in_band = (jnp.abs(rel) <= WINDOW) & (col >= lo) & (col < hi)
        valid = jnp.where(col >= 3 * BLOCK, 1, in_band.astype(jnp.int32))
        valid = jnp.concatenate([valid] * group, axis=0)
        s = jnp.where(valid != 0, s, NEG_INF)
    sink = jnp.concatenate(
        [jnp.full((BLOCK, 1), sink_ref[h * group + g], F32) for g in range(group)], axis=0)
    m = jnp.maximum(jnp.max(s, axis=-1, keepdims=True), sink)
    p = jnp.exp(s - m)
    den = jnp.sum(p, axis=-1, keepdims=True) + jnp.exp(sink - m)
    p = p / den
    o = jnp.dot(p.astype(vall.dtype), vall, preferred_element_type=F32)
    for g in range(group):
        o_ref[:, g * HEAD_DIM:(g + 1) * HEAD_DIM] = o[g * BLOCK:(g + 1) * BLOCK].astype(o_ref.dtype)


def _attention(z_qk, z_rest, sinks, att_prev, dims, v_col0, context_queries):
    b, s, c = dims["B"], dims["S"], dims["C"]
    group = N_Q_HEADS // N_KV_HEADS
    qw = group * HEAD_DIM
    kcol0 = N_Q_HEADS
    vcol0 = v_col0 // HEAD_DIM
    ctx_row0 = (b * s) // c

    def ctx_k(bi, n, h):
        return (ctx_row0 + bi, kcol0 + h)

    def ctx_v(bi, n, h):
        return (ctx_row0 + bi, vcol0 + h)

    if context_queries:
        n_blocks = c // BLOCK
        q_row0 = (b * s) // BLOCK

        def q_map(bi, n, h):
            return (q_row0 + bi * n_blocks + n, h)

        in_specs = [pl.BlockSpec((BLOCK, qw), q_map),
                    pl.BlockSpec((c, HEAD_DIM), ctx_k), pl.BlockSpec((c, HEAD_DIM), ctx_v)]
        args = [z_qk, z_qk, z_rest]
    else:
        n_blocks = s // BLOCK

        def q_map(bi, n, h):
            return (bi * n_blocks + n, h)

        def nb_map(off, col0):
            def f(bi, n, h):
                return (bi * n_blocks + jnp.clip(n + off, 0, n_blocks - 1), col0 + h)
            return f

        blk = (BLOCK, HEAD_DIM)
        in_specs = [pl.BlockSpec((BLOCK, qw), q_map)]
        in_specs += [pl.BlockSpec(blk, nb_map(off, kcol0)) for off in (-1, 0, 1)]
        in_specs += [pl.BlockSpec((c, HEAD_DIM), ctx_k)]
        in_specs += [pl.BlockSpec(blk, nb_map(off, vcol0)) for off in (-1, 0, 1)]
        in_specs += [pl.BlockSpec((c, HEAD_DIM), ctx_v)]
        args = [z_qk, z_qk, z_qk, z_qk, z_qk, z_rest, z_rest, z_rest, z_rest]

    n_in = 1 + len(args)
    return pl.pallas_call(
        functools.partial(_attn_kernel, n_blocks=n_blocks, group=group,
                          has_local=not context_queries),
        out_shape=jax.ShapeDtypeStruct(att_prev.shape, BF16),
        grid=(b, n_blocks, N_KV_HEADS),
        in_specs=[pl.BlockSpec(memory_space=pltpu.SMEM)] + in_specs
                 + [pl.BlockSpec(memory_space=pl.ANY)],
        out_specs=pl.BlockSpec((BLOCK, qw), q_map),
        input_output_aliases={n_in: 0},
        compiler_params=_params("arbitrary", "arbitrary", "arbitrary"),
        name="attention_ctx" if context_queries else "attention",
    )(sinks, *args, att_prev)


def _softplus(x):
    return jnp.maximum(x, 0.0) + jnp.log1p(jnp.exp(-jnp.abs(x)))


def _lru_kernel(perm_ref, permt_ref,
                uf_ref, ufp_ref, ufn_ref, ub_ref, ubp_ref, ubn_ref,
                cw_ref, cb_ref, wr_ref, br_ref, wi_ref, bi_ref, lam_ref,
                hf_ref, hb_ref,
                ext_s, a_s, b_s, end_s, st_s, *, n_chunks):
    s = pl.program_id(1)
    tc, w = uf_ref.shape
    sub = tc // SUBLANES
    n_lru_blocks = w // LRU_BLOCK_DIM

    @pl.when(s == 0)
    def _():
        st_s[...] = jnp.zeros_like(st_s)

    row8 = lax.broadcasted_iota(jnp.int32, (SUBLANES, w), 0)
    is_ctx = s == 0
    for d in range(2):
        if d == 0:
            u_ref, p_ref, n_ref, o_ref = uf_ref, ufp_ref, ufn_ref, hf_ref
            at_start = is_ctx | (s == 1)
            at_end = is_ctx | (s == n_chunks)
        else:
            u_ref, p_ref, n_ref, o_ref = ub_ref, ubp_ref, ubn_ref, hb_ref
            at_start = is_ctx | (s == n_chunks)
            at_end = is_ctx | (s == 1)

        up = jnp.dot(perm_ref[...], u_ref[...], preferred_element_type=F32)
        prev = jnp.where(at_start, 0.0, p_ref[...].astype(F32))
        nxt = jnp.where(at_end, 0.0, n_ref[...].astype(F32))
        last = up[(sub - 1) * SUBLANES:]
        last2 = up[(sub - 2) * SUBLANES:(sub - 1) * SUBLANES]
        first = up[:SUBLANES]
        ext_s[0:SUBLANES] = jnp.where(row8 == 0, prev[SUBLANES - 2:SUBLANES - 1],
                                      pltpu.roll(last2, 1, 0))
        ext_s[SUBLANES:2 * SUBLANES] = jnp.where(row8 == 0, prev[SUBLANES - 1:SUBLANES],
                                                 pltpu.roll(last, 1, 0))
        ext_s[2 * SUBLANES:2 * SUBLANES + tc] = up
        ext_s[2 * SUBLANES + tc:] = jnp.where(row8 == SUBLANES - 1, nxt[0:1],
                                              pltpu.roll(first, SUBLANES - 1, 0))

        for blk in range(n_lru_blocks):
            sl = slice(blk * LRU_BLOCK_DIM, (blk + 1) * LRU_BLOCK_DIM)
            uc = cb_ref[:, sl]
            for i in range(CONV_WIDTH):
                uc = uc + ext_s[i * SUBLANES:i * SUBLANES + tc, sl] * cw_ref[i:i + 1, sl]
            ucb = uc.astype(BF16)
            r = jax.nn.sigmoid(jnp.dot(ucb, wr_ref[d, blk], preferred_element_type=F32)
                               + br_ref[d:d + 1, sl])
            gi = jax.nn.sigmoid(jnp.dot(ucb, wi_ref[d, blk], preferred_element_type=F32)
                                + bi_ref[d:d + 1, sl])
            log_a = (-LRU_C * r) * _softplus(-lam_ref[d:d + 1, sl])
            a = jnp.exp(log_a)
            mult = jnp.sqrt(1.0 - jnp.exp(2.0 * log_a))
            a_s[:, sl] = a
            b_s[:, sl] = mult * (gi * uc)

        cols = min(4 * LANES, w)
        steps = range(sub) if d == 0 else range(sub - 1, -1, -1)
        for c0 in range(0, w, cols):
            cs = slice(c0, c0 + cols)
            hloc = jnp.zeros((SUBLANES, cols), F32)
            prod = jnp.ones((SUBLANES, cols), F32)
            for t in steps:
                rows = slice(t * SUBLANES, (t + 1) * SUBLANES)
                at = a_s[rows, cs]
                hloc = at * hloc + b_s[rows, cs]
                prod = at * prod
                b_s[rows, cs] = hloc
                a_s[rows, cs] = prod
            end_s[0:SUBLANES, cs] = hloc
            end_s[SUBLANES:, cs] = prod

        hend = end_s[0:SUBLANES]
        pend = end_s[SUBLANES:]
        carry = st_s[d:d + 1]
        carries = [None] * SUBLANES
        order = range(SUBLANES) if d == 0 else range(SUBLANES - 1, -1, -1)
        for j in order:
            carries[j] = carry
            carry = hend[j:j + 1] + pend[j:j + 1] * carry
        st_s[d:d + 1] = carry
        ctile = jnp.concatenate(carries, axis=0)
        for t in range(sub):
            rows = slice(t * SUBLANES, (t + 1) * SUBLANES)
            b_s[rows] = b_s[rows] + a_s[rows] * ctile
        o_ref[...] = jnp.dot(permt_ref[...], b_s[...].astype(BF16),
                             preferred_element_type=F32).astype(o_ref.dtype)


def _lru(z_rest, u_col0, conv_w, conv_b, wr, br, wi, bi, lam, dims):
    b, s, c = dims["B"], dims["S"], dims["C"]
    r_rows = z_rest.shape[0]
    w = conv_w.shape[1]
    tc = c
    sub = tc // SUBLANES
    assert tc % (2 * SUBLANES) == 0 and s % tc == 0 and u_col0 % w == 0
    n_chunks = s // tc
    ucol = u_col0 // w
    ctx_blk0 = (b * s) // tc
    halo_per_chunk = tc // SUBLANES
    n_halo = r_rows // SUBLANES

    def fwd_blk(bi_, st):
        return jnp.where(st == 0, ctx_blk0 + bi_, bi_ * n_chunks + st - 1)

    def bwd_blk(bi_, st):
        return jnp.where(st == 0, ctx_blk0 + bi_, bi_ * n_chunks + n_chunks - st)

    def main(blk):
        return lambda bi_, st: (blk(bi_, st), ucol)

    def prev(blk):
        return lambda bi_, st: (jnp.maximum(blk(bi_, st) * halo_per_chunk - 1, 0),
                                ucol)

    def nxt(blk):
        return lambda bi_, st: (jnp.minimum((blk(bi_, st) + 1) * halo_per_chunk, n_halo - 1),
                                ucol)

    def out_map(blk):
        return lambda bi_, st: (blk(bi_, st), 0)

    def full(shape):
        return pl.BlockSpec(shape, lambda bi_, st: (0,) * len(shape))

    t_idx = jnp.arange(tc)
    src = (t_idx % SUBLANES) * sub + t_idx // SUBLANES
    perm = (src[:, None] == t_idx[None, :]).astype(BF16)
    permt = perm.T

    n_blk = w // LRU_BLOCK_DIM
    out = pl.pallas_call(
        functools.partial(_lru_kernel, n_chunks=n_chunks),
        out_shape=[jax.ShapeDtypeStruct((r_rows, w), BF16)] * 2,
        grid=(b, n_chunks + 1),
        in_specs=[full((tc, tc)), full((tc, tc)),
                  pl.BlockSpec((tc, w), main(fwd_blk)),
                  pl.BlockSpec((SUBLANES, w), prev(fwd_blk)),
                  pl.BlockSpec((SUBLANES, w), nxt(fwd_blk)),
                  pl.BlockSpec((tc, w), main(bwd_blk)),
                  pl.BlockSpec((SUBLANES, w), prev(bwd_blk)),
                  pl.BlockSpec((SUBLANES, w), nxt(bwd_blk)),
                  full((CONV_WIDTH, w)), full((1, w)),
                  full((2, n_blk, LRU_BLOCK_DIM, LRU_BLOCK_DIM)), full((2, w)),
                  full((2, n_blk, LRU_BLOCK_DIM, LRU_BLOCK_DIM)), full((2, w)),
                  full((2, w))],
        out_specs=[pl.BlockSpec((tc, w), out_map(fwd_blk)),
                   pl.BlockSpec((tc, w), out_map(bwd_blk))],
        scratch_shapes=[pltpu.VMEM((tc + 3 * SUBLANES, w), F32),
                        pltpu.VMEM((tc, w), F32),
                        pltpu.VMEM((tc, w), F32),
                        pltpu.VMEM((2 * SUBLANES, w), F32),
                        pltpu.VMEM((2, w), F32)],
        compiler_params=_params("arbitrary", "arbitrary"),
        name="conv_rglru",
    )(perm, permt, z_rest, z_rest, z_rest, z_rest, z_rest, z_rest,
      conv_w, conv_b.reshape(1, w), wr, br, wi, bi, lam)
    return out


def _gelu_tanh(x):
    return 0.5 * x * (1.0 + jnp.tanh(math.sqrt(2.0 / math.pi) * (x + 0.044715 * (x * x * x))))


def _merge_kernel(att_ref, hf_ref, hb_ref, gl_ref, woa_ref, wob_ref, ga_ref, gb_ref, o_ref, rec_s):
    @pl.when(pl.program_id(1) == 0)
    def _():
        rec = hf_ref[...].astype(F32) + hb_ref[...].astype(F32)
        rec_s[...] = (rec * _gelu_tanh(gl_ref[...].astype(F32))).astype(rec_s.dtype)

    ya = jnp.dot(att_ref[...], woa_ref[...], preferred_element_type=F32)
    yb = jnp.dot(rec_s[...], wob_ref[...], preferred_element_type=F32)
    out = jax.nn.sigmoid(ga_ref[...].astype(F32)) * ya + jax.nn.sigmoid(gb_ref[...].astype(F32)) * yb
    o_ref[...] = out.astype(o_ref.dtype)


def _merge(att, hf, hb, z_rest, cols, w_oa, w_ob, tm, n_rows):
    r, qd = att.shape
    w = hf.shape[1]
    d = w_oa.shape[1]
    tn = _tile(d, 512)
    assert cols["gl"] % w == 0 and cols["ga"] % tn == 0 and cols["gb"] % tn == 0
    gl_blk, ga_blk, gb_blk = cols["gl"] // w, cols["ga"] // tn, cols["gb"] // tn
    return pl.pallas_call(
        _merge_kernel,
        out_shape=jax.ShapeDtypeStruct((r, d), BF16),
        grid=(n_rows // tm, d // tn),
        in_specs=[pl.BlockSpec((tm, qd), lambda i, j: (i, 0)),
                  pl.BlockSpec((tm, w), lambda i, j: (i, 0)),
                  pl.BlockSpec((tm, w), lambda i, j: (i, 0)),
                  pl.BlockSpec((tm, w), lambda i, j: (i, gl_blk)),
                  pl.BlockSpec((qd, tn), lambda i, j: (0, j)),
                  pl.BlockSpec((w, tn), lambda i, j: (0, j)),
                  pl.BlockSpec((tm, tn), lambda i, j: (i, ga_blk + j)),
                  pl.BlockSpec((tm, tn), lambda i, j: (i, gb_blk + j))],
        out_specs=pl.BlockSpec((tm, tn), lambda i, j: (i, j)),
        scratch_shapes=[pltpu.VMEM((tm, w), BF16)],
        compiler_params=_params("arbitrary", "arbitrary"),
        name="gated_merge",
    )(att, hf, hb, z_rest, w_oa, w_ob, z_rest, z_rest)


def _moe_kernel(h_ref, comb_ref, w1_ref, b1_ref, w2_ref, b2_ref, o_ref, acc_ref, *, expert_dim):
    e = pl.program_id(1)
    comb = comb_ref[...]

    @pl.when(e == 0)
    def _():
        acc_ref[...] = jnp.dot(comb, b2_ref[...], precision=HIGHEST, preferred_element_type=F32)

    hdn = jnp.dot(h_ref[...], w1_ref[...], preferred_element_type=F32) + b1_ref[...]
    glu = jnp.minimum(hdn[:, :expert_dim], SWIGLU_LIMIT)
    lin = jnp.clip(hdn[:, expert_dim:], -SWIGLU_LIMIT, SWIGLU_LIMIT)
    act = glu * jax.nn.sigmoid(SWIGLU_ALPHA * glu) * (lin + 1.0)
    lane = lax.broadcasted_iota(jnp.int32, comb.shape, 1)
    ce = jnp.sum(jnp.where(lane == e, comb, 0.0), axis=-1, keepdims=True)
    act = act * ce
    acc_ref[...] += jnp.dot(act.astype(BF16), w2_ref[...], preferred_element_type=F32)

    @pl.when(e == pl.num_programs(1) - 1)
    def _():
        o_ref[...] = acc_ref[...].astype(o_ref.dtype)


def _moe(h, comb, w1, b1, w2, b2, tm, n_rows):
    r, d = h.shape
    n_e, _, f2 = w1.shape
    f = f2 // 2
    return pl.pallas_call(
        functools.partial(_moe_kernel, expert_dim=f),
        out_shape=jax.ShapeDtypeStruct((r, d), BF16),
        grid=(n_rows // tm, n_e),
        in_specs=[pl.BlockSpec((tm, d), lambda i, e: (i, 0)),
                  pl.BlockSpec((tm, n_e), lambda i, e: (i, 0)),
                  pl.BlockSpec((None, d, f2), lambda i, e: (e, 0, 0)),
                  pl.BlockSpec((None, 1, f2), lambda i, e: (e, 0, 0)),
                  pl.BlockSpec((None, f, d), lambda i, e: (e, 0, 0)),
                  pl.BlockSpec((n_e, d), lambda i, e: (0, 0))],
        out_specs=pl.BlockSpec((tm, d), lambda i, e: (i, 0)),
        scratch_shapes=[pltpu.VMEM((tm, d), F32)],
        compiler_params=_params("arbitrary", "arbitrary"),
        name="moe_dense",
    )(h, comb, w1, b1.reshape(n_e, 1, f2), w2, b2)


def _rope_tables(b, s, c):
    rows = jnp.arange(s)
    pos_r = (rows // GRID_W).astype(F32)
    pos_c = (rows % GRID_W).astype(F32)
    axis_dim = HEAD_DIM // 2
    inv = ROPE_THETA ** (-jnp.arange(0, axis_dim, 2, dtype=F32) / axis_dim)
    ang_r = pos_r[:, None] * inv[None, :]
    ang_c = pos_c[:, None] * inv[None, :]
    ang = jnp.concatenate([ang_r, ang_r, ang_c, ang_c], axis=-1)
    sign = jnp.where((jnp.arange(HEAD_DIM) % (HEAD_DIM // 2)) < HEAD_DIM // 4, -1.0, 1.0)
    cos = jnp.concatenate([jnp.tile(jnp.cos(ang), (b, 1)), jnp.ones((b * c, HEAD_DIM), F32)], axis=0)
    sin = jnp.concatenate([jnp.tile(jnp.sin(ang) * sign, (b, 1)),
                           jnp.zeros((b * c, HEAD_DIM), F32)], axis=0)
    return cos, sin


def kernel(x, c, ctx, c_ctx, w_mod1, w_mod2, b_mod, g_mix, g_ffn, w_in, conv_w, conv_b, lru_wr,
           lru_br, lru_wi, lru_bi, lru_lam, sinks, w_oa, w_ob, w_out, w_router, b_router, w_exp1,
           b_exp1, w_exp2, b_exp2, g_final):
    b, s, d = x.shape
    cl = ctx.shape[1]
    depth = w_in.shape[0]
    lw = conv_w.shape[2]
    q_dim = N_Q_HEADS * HEAD_DIM
    kv_dim = N_KV_HEADS * HEAD_DIM
    n_lat = b * s
    n_all = n_lat + b * cl
    dims = {"B": b, "S": s, "C": cl}
    assert s % BLOCK == 0 and cl % BLOCK == 0 and n_lat % cl == 0

    tm = _tile(math.gcd(s, b * cl), 512)
    tm_norm = _tile(tm, 256)

    def group_of(tile):
        def f(i):
            return jnp.where(i * tile < n_lat, (i * tile) // s, b)
        return f

    g8 = -(-(b + 1) // SUBLANES) * SUBLANES
    cond = jnp.zeros((g8, d), F32).at[:b].set(c).at[b].set(c_ctx)
    xa = jnp.concatenate([x.reshape(n_lat, d), ctx.reshape(b * cl, d)], axis=0)
    cos, sin = _rope_tables(b, s, cl)

    o_q, o_k, o_v, o_u, o_gl, o_ga = (0, q_dim, q_dim + kv_dim, q_dim + 2 * kv_dim,
                                      q_dim + 2 * kv_dim + lw, q_dim + 2 * kv_dim + 2 * lw)
    o_gb = o_ga + d
    rest_cols = {"u": 0, "gl": lw, "ga": 2 * lw, "gb": 2 * lw + d, "v": 2 * lw + 2 * d}

    att = jnp.zeros((n_all, q_dim), BF16)
    delta = None
    for l in range(depth):
        last = l == depth - 1
        n_rows = n_lat if last else n_all
        mods = _modulation(cond, w_mod1[l], w_mod2[l], b_mod[l]).reshape(g8 * N_MOD, 1, d)

        w_qk = w_in[l][:, :o_v].astype(BF16)
        w_rest = jnp.concatenate([w_in[l][:, o_u:o_gl], w_in[l][:, o_gl:o_ga], w_in[l][:, o_ga:o_gb],
                                  w_in[l][:, o_gb:], w_in[l][:, o_v:o_u]], axis=1).astype(BF16)

        if delta is None:
            xa, h = _resnorm(xa, mods, group_of(tm_norm), tm_norm, n_all, g_mix[l], 0, 1)
        else:
            xa, h = _resnorm(xa, mods, group_of(tm_norm), tm_norm, n_all, g_mix[l], 0, 1,
                             delta=delta, gate_mods=mods_prev, gate_idx=5)

        z_qk = _matmul(h, w_qk, tm, n_all, rope=(cos, sin))
        z_rest = _matmul(h, w_rest, tm, n_all)

        att = _attention(z_qk, z_rest, sinks[l], att, dims, rest_cols["v"], context_queries=False)
        if not last:
            att = _attention(z_qk, z_rest, sinks[l], att, dims, rest_cols["v"], context_queries=True)
        hf, hb = _lru(z_rest, rest_cols["u"], conv_w[l], conv_b[l], lru_wr[l].astype(BF16), lru_br[l],
                      lru_wi[l].astype(BF16), lru_bi[l], lru_lam[l], dims)
        merged = _merge(att, hf, hb, z_rest, rest_cols, w_oa[l].astype(BF16), w_ob[l].astype(BF16),
                        tm, n_rows)
        y = _matmul(merged, w_out[l].astype(BF16), tm, n_rows)
        xa, h2, comb = _resnorm(xa, mods, group_of(tm_norm), tm_norm, n_rows, g_ffn[l], 3, 4,
                                delta=y, gate_mods=mods, gate_idx=2, router=(w_router[l], b_router[l]))
        tm_moe = _tile(tm, 256)
        delta = _moe(h2, comb, w_exp1[l].astype(BF16), b_exp1[l], w_exp2[l].astype(BF16), b_exp2[l],
                     tm_moe, n_rows)
        mods_prev = mods

    out = _final_norm(xa, delta, mods_prev, group_of(tm_norm), tm_norm, n_lat, g_final, 5)
    return out.reshape(b, s, d)
```

```python
import functools
import math

import jax
import jax.numpy as jnp
from jax import lax
from jax.experimental import pallas as pl
from jax.experimental.pallas import tpu as pltpu

GRID_W = 64
N_Q_HEADS = 16
N_KV_HEADS = 4
HEAD_DIM = 128
WINDOW = 128
BLOCK = 128
ROPE_THETA = 10000.0
NEG_INF = -1e30
LRU_BLOCK_DIM = 128
CONV_WIDTH = 4
LRU_C = 8.0
TOP_K = 4
SWIGLU_LIMIT = 7.0
SWIGLU_ALPHA = 1.702
N_MOD = 6
EPS = 1e-6

V7X_VMEM_LIMIT_BYTES = 56 * 1024 * 1024
SUBLANES = 8
LANES = 128

F32 = jnp.float32
BF16 = jnp.bfloat16
HIGHEST = lax.Precision.HIGHEST


def _params(*sem):
    return pltpu.CompilerParams(dimension_semantics=sem, vmem_limit_bytes=V7X_VMEM_LIMIT_BYTES)


def _tile(n, pref):
    if n <= pref:
        return n
    t = pref
    while t >= 8:
        if n % t == 0 and t % 8 == 0:
            return t
        t -= 8
    return n


def _mod_kernel(cond_ref, w1_ref, w2_ref, b_ref, o_ref, t_ref):
    @pl.when(pl.program_id(0) == 0)
    def _():
        cnd = cond_ref[...]
        t_ref[...] = jnp.dot(cnd * jax.nn.sigmoid(cnd), w1_ref[...], precision=HIGHEST,
                             preferred_element_type=F32)

    o_ref[...] = jnp.dot(t_ref[...], w2_ref[...], precision=HIGHEST,
                         preferred_element_type=F32) + b_ref[...]


def _modulation(cond, w1, w2, b):
    g8, d = cond.shape
    mr = w1.shape[1]
    n = w2.shape[1]
    tn = _tile(n, 2048)
    return pl.pallas_call(
        _mod_kernel,
        out_shape=jax.ShapeDtypeStruct((g8, n), F32),
        grid=(n // tn,),
        in_specs=[pl.BlockSpec((g8, d), lambda j: (0, 0)),
                  pl.BlockSpec((d, mr), lambda j: (0, 0)),
                  pl.BlockSpec((mr, tn), lambda j: (0, j)),
                  pl.BlockSpec((1, tn), lambda j: (0, j))],
        out_specs=pl.BlockSpec((g8, tn), lambda j: (0, j)),
        scratch_shapes=[pltpu.VMEM((g8, mr), F32)],
        compiler_params=_params("arbitrary"),
        name="modulation",
    )(cond, w1, w2, b.reshape(1, n))


def _pack_halves(v):
    half = v.shape[1] // 2
    lo = lax.bitcast_convert_type(v[:, :half].astype(BF16).astype(F32), jnp.uint32)
    hi = lax.bitcast_convert_type(v[:, half:].astype(BF16).astype(F32), jnp.uint32)
    return (hi & jnp.uint32(0xFFFF0000)) | (lo >> jnp.uint32(16))


def _unpack_halves(p):
    lo = lax.bitcast_convert_type(p << jnp.uint32(16), F32)
    hi = lax.bitcast_convert_type(p & jnp.uint32(0xFFFF0000), F32)
    return lo, hi


def _resnorm_kernel(*refs, has_delta, with_router, n_experts, capacity):
    refs = list(refs)
    x_ref = refs.pop(0)
    if has_delta:
        d_ref = refs.pop(0)
        gate_ref = refs.pop(0)
    g_ref, sh_ref, sc_ref = refs.pop(0), refs.pop(0), refs.pop(0)
    if with_router:
        wr_ref, br_ref = refs.pop(0), refs.pop(0)
    xo_ref, h_ref = refs.pop(0), refs.pop(0)

    x = x_ref[...]
    if has_delta:
        x = x + gate_ref[...] * d_ref[...].astype(F32)
    xo_ref[...] = x
    ms = jnp.mean(x * x, axis=-1, keepdims=True)
    y = (x * lax.rsqrt(ms + EPS)) * g_ref[...]
    h = y * (1.0 + sc_ref[...]) + sh_ref[...]
    if not with_router:
        h_ref[...] = h.astype(h_ref.dtype)
        return

    h_ref[...] = _pack_halves(h)
    comb_ref, pos_ref, wts_ref, cnt_ref, run_s = refs
    tm = h.shape[0]

    @pl.when(pl.program_id(0) == 0)
    def _():
        run_s[...] = jnp.zeros_like(run_s)

    logits = jnp.dot(h, wr_ref[...], precision=HIGHEST, preferred_element_type=F32) + br_ref[...]
    lane = lax.broadcasted_iota(jnp.int32, logits.shape, 1)
    work = logits
    comb = jnp.zeros_like(logits)
    chosen = jnp.zeros_like(logits)
    denom = jnp.zeros((tm, 1), F32)
    top0 = None
    picks = []
    for k in range(TOP_K):
        m = jnp.max(work, axis=-1, keepdims=True)
        idx = jnp.min(jnp.where(work == m, lane, n_experts), axis=-1, keepdims=True)
        onehot = lane == idx
        if k == 0:
            top0 = m
        e = jnp.exp(m - top0)
        comb = comb + jnp.where(onehot, e, 0.0)
        chosen = chosen + jnp.where(onehot, 1.0, 0.0)
        denom = denom + e
        work = jnp.where(onehot, -jnp.inf, work)
        picks.append((idx, onehot, e))
    comb_ref[...] = comb / denom

    earlier = (lax.broadcasted_iota(jnp.int32, (tm, tm), 0)
               > lax.broadcasted_iota(jnp.int32, (tm, tm), 1))
    before = run_s[...] + jnp.dot(jnp.where(earlier, 1.0, 0.0).astype(BF16), chosen.astype(BF16),
                                  preferred_element_type=F32)
    lane_k = lax.broadcasted_iota(jnp.int32, (tm, TOP_K), 1)
    pos = jnp.zeros((tm, TOP_K), jnp.int32)
    wts = jnp.zeros((tm, TOP_K), F32)
    for k, (idx, onehot, e) in enumerate(picks):
        rank = jnp.sum(jnp.where(onehot, before, 0.0), axis=-1, keepdims=True).astype(jnp.int32)
        pos = jnp.where(lane_k == k, idx * capacity + rank, pos)
        wts = jnp.where(lane_k == k, e / denom, wts)
    pos_ref[...] = pos
    wts_ref[...] = wts
    run_s[...] = run_s[...] + jnp.sum(chosen, axis=0, keepdims=True)
    cnt_ref[...] = run_s[...].astype(jnp.int32)


def _resnorm(x, mods, group_of_block, tm, n_rows, g, shift_idx, scale_idx, *, delta=None,
             gate_mods=None, gate_idx=None, router=None, capacity=0):
    r, d = x.shape
    has_delta = delta is not None
    with_router = router is not None

    def row(i):
        return (i, 0)

    def mod_spec(m):
        return pl.BlockSpec((None, 1, d), lambda i: (group_of_block(i) * N_MOD + m, 0, 0))

    in_specs = [pl.BlockSpec((tm, d), row)]
    args = [x]
    if has_delta:
        in_specs += [pl.BlockSpec((tm, d), row), mod_spec(gate_idx)]
        args += [delta, gate_mods]
    in_specs += [pl.BlockSpec((1, d), lambda i: (0, 0)), mod_spec(shift_idx), mod_spec(scale_idx)]
    args += [g.reshape(1, d), mods, mods]
    out_shape = [jax.ShapeDtypeStruct((r, d), F32)]
    out_specs = [pl.BlockSpec((tm, d), row)]
    scratch = []
    n_experts = 0
    if with_router:
        w_router, b_router = router
        n_experts = w_router.shape[1]
        in_specs += [pl.BlockSpec((d, n_experts), lambda i: (0, 0)),
                     pl.BlockSpec((1, n_experts), lambda i: (0, 0))]
        args += [w_router, b_router.reshape(1, n_experts)]
        out_shape += [jax.ShapeDtypeStruct((r, d // 2), jnp.uint32),
                      jax.ShapeDtypeStruct((r, n_experts), F32),
                      jax.ShapeDtypeStruct((r, TOP_K), jnp.int32),
                      jax.ShapeDtypeStruct((r, TOP_K), F32),
                      jax.ShapeDtypeStruct((1, n_experts), jnp.int32)]
        out_specs += [pl.BlockSpec((tm, d // 2), row),
                      pl.BlockSpec((tm, n_experts), row),
                      pl.BlockSpec((tm, TOP_K), row),
                      pl.BlockSpec((tm, TOP_K), row),
                      pl.BlockSpec((1, n_experts), lambda i: (0, 0))]
        scratch = [pltpu.VMEM((1, n_experts), F32)]
    else:
        out_shape.append(jax.ShapeDtypeStruct((r, d), BF16))
        out_specs.append(pl.BlockSpec((tm, d), row))
    return pl.pallas_call(
        functools.partial(_resnorm_kernel, has_delta=has_delta, with_router=with_router,
                          n_experts=n_experts, capacity=capacity),
        out_shape=out_shape,
        grid=(n_rows // tm,),
        in_specs=in_specs,
        out_specs=out_specs,
        scratch_shapes=scratch,
        input_output_aliases={0: 0},
        compiler_params=_params("arbitrary"),
        name="resnorm_router" if with_router else "resnorm",
    )(*args)


def _final_kernel(x_ref, d_ref, gate_ref, g_ref, o_ref):
    x = x_ref[...] + gate_ref[...] * d_ref[...].astype(F32)
    ms = jnp.mean(x * x, axis=-1, keepdims=True)
    o_ref[...] = (x * lax.rsqrt(ms + EPS)) * g_ref[...]


def _final_norm(x, delta, mods, group_of_block, tm, n_rows, g, gate_idx):
    r, d = x.shape
    return pl.pallas_call(
        _final_kernel,
        out_shape=jax.ShapeDtypeStruct((n_rows, d), F32),
        grid=(n_rows // tm,),
        in_specs=[pl.BlockSpec((tm, d), lambda i: (i, 0)),
                  pl.BlockSpec((tm, d), lambda i: (i, 0)),
                  pl.BlockSpec((None, 1, d), lambda i: (group_of_block(i) * N_MOD + gate_idx, 0, 0)),
                  pl.BlockSpec((1, d), lambda i: (0, 0))],
        out_specs=pl.BlockSpec((tm, d), lambda i: (i, 0)),
        compiler_params=_params("arbitrary"),
        name="final_norm",
    )(x, delta, mods, g.reshape(1, d))


def _mm_kernel(a_ref, w_ref, o_ref, wb_s):
    @pl.when(pl.program_id(1) == 0)
    def _():
        wb_s[...] = w_ref[...].astype(BF16)

    o_ref[...] = jnp.dot(a_ref[...], wb_s[...], preferred_element_type=F32).astype(o_ref.dtype)


def _mm_rope_kernel(a_ref, w_ref, cos_ref, sin_ref, o_ref, wb_s):
    @pl.when(pl.program_id(1) == 0)
    def _():
        wb_s[...] = w_ref[...].astype(BF16)

    acc = jnp.dot(a_ref[...], wb_s[...], preferred_element_type=F32)
    cos = cos_ref[...]
    sin = sin_ref[...]
    lane = lax.broadcasted_iota(jnp.int32, cos.shape, 1)
    first_half = (lane % (HEAD_DIM // 2)) < (HEAD_DIM // 4)
    for c in range(acc.shape[1] // HEAD_DIM):
        xc = acc[:, c * HEAD_DIM:(c + 1) * HEAD_DIM]
        rot = jnp.where(first_half, pltpu.roll(xc, HEAD_DIM - HEAD_DIM // 4, 1),
                        pltpu.roll(xc, HEAD_DIM // 4, 1))
        o_ref[:, c * HEAD_DIM:(c + 1) * HEAD_DIM] = (xc * cos + rot * sin).astype(o_ref.dtype)


def _matmul(a, w_stack, layer, col_blocks, tm, n_rows, tn, rope=None):
    r, k = a.shape
    n_col = col_blocks[0]
    col_of = col_blocks[1]
    in_specs = [pl.BlockSpec((tm, k), lambda j, i: (i, 0)),
                pl.BlockSpec((None, k, tn), lambda j, i: (layer, 0, col_of(j)))]
    args = [a, w_stack]
    kern = _mm_kernel
    if rope is not None:
        in_specs += [pl.BlockSpec((tm, HEAD_DIM), lambda j, i: (i, 0)),
                     pl.BlockSpec((tm, HEAD_DIM), lambda j, i: (i, 0))]
        args += list(rope)
        kern = _mm_rope_kernel
    return pl.pallas_call(
        kern,
        out_shape=jax.ShapeDtypeStruct((r, n_col * tn), BF16),
        grid=(n_col, n_rows // tm),
        in_specs=in_specs,
        out_specs=pl.BlockSpec((tm, tn), lambda j, i: (i, j)),
        scratch_shapes=[pltpu.VMEM((k, tn), BF16)],
        compiler_params=_params("arbitrary", "arbitrary"),
        name="matmul_rope" if rope is not None else "matmul",
    )(*args)


def _attn_kernel(sink_ref, q_ref, *refs, n_blocks, group, has_local):
    o_ref = refs[-1]
    n = pl.program_id(1)
    h = pl.program_id(2)
    q = q_ref[...]
    qs = jnp.concatenate([q[:, g * HEAD_DIM:(g + 1) * HEAD_DIM] for g in range(group)], axis=0)
    if has_local:
        kp, ko, kn, kc, vp, vo, vn, vc = [ref[...] for ref in refs[:8]]
        kall = jnp.concatenate([kp, ko, kn, kc], axis=0)
        vall = jnp.concatenate([vp, vo, vn, vc], axis=0)
    else:
        kall = refs[0][...]
        vall = refs[1][...]
    s = lax.dot_general(qs, kall, (((1,), (1,)), ((), ())), preferred_element_type=F32)
    s = s * (HEAD_DIM ** -0.5)
    if has_local:
        n_keys = kall.shape[0]
        t = lax.broadcasted_iota(jnp.int32, (BLOCK, n_keys), 0)
        col = lax.broadcasted_iota(jnp.int32, (BLOCK, n_keys), 1)
        rel = col - BLOCK - t
        lo = jnp.where(n > 0, 0, BLOCK)
        hi = jnp.where(n < n_blocks - 1, 3 * BLOCK, 2 * BLOCK)
        in_band = (jnp.abs(rel) <= WINDOW) & (col >= lo) & (col < hi)
        valid = jnp.where(col >= 3 * BLOCK, 1, in_band.astype(jnp.int32))
        valid = jnp.concatenate([valid] * group, axis=0)
        s = jnp.where(valid != 0, s, NEG_INF)
    sink = jnp.concatenate(
        [jnp.full((BLOCK, 1), sink_ref[h * group + g], F32) for g in range(group)], axis=0)
    m = jnp.maximum(jnp.max(s, axis=-1, keepdims=True), sink)
    p = jnp.exp(s - m)
    den = jnp.sum(p, axis=-1, keepdims=True) + jnp.exp(sink - m)
    p = p / den
    o = jnp.dot(p.astype(vall.dtype), vall, preferred_element_type=F32)
    for g in range(group):
        o_ref[:, g * HEAD_DIM:(g + 1) * HEAD_DIM] = o[g * BLOCK:(g + 1) * BLOCK].astype(o_ref.dtype)


def _attention(z_qk, z_rest, sinks, att_prev, dims, v_col0, context_queries):
    b, s, c = dims["B"], dims["S"], dims["C"]
    group = N_Q_HEADS // N_KV_HEADS
    qw = group * HEAD_DIM
    kcol0 = N_Q_HEADS
    vcol0 = v_col0 // HEAD_DIM
    ctx_row0 = (b * s) // c

    def ctx_k(bi, n, h):
        return (ctx_row0 + bi, kcol0 + h)

    def ctx_v(bi, n, h):
        return (ctx_row0 + bi, vcol0 + h)

    if context_queries:
        n_blocks = c // BLOCK
        q_row0 = (b * s) // BLOCK

        def q_map(bi, n, h):
            return (q_row0 + bi * n_blocks + n, h)

        in_specs = [pl.BlockSpec((BLOCK, qw), q_map),
                    pl.BlockSpec((c, HEAD_DIM), ctx_k), pl.BlockSpec((c, HEAD_DIM), ctx_v)]
        args = [z_qk, z_qk, z_rest]
    else:
        n_blocks = s // BLOCK

        def q_map(bi, n, h):
            return (bi * n_blocks + n, h)

        def nb_map(off, col0):
            def f(bi, n, h):
                return (bi * n_blocks + jnp.clip(n + off, 0, n_blocks - 1), col0 + h)
            return f

        blk = (BLOCK, HEAD_DIM)
        in_specs = [pl.BlockSpec((BLOCK, qw), q_map)]
        in_specs += [pl.BlockSpec(blk, nb_map(off, kcol0)) for off in (-1, 0, 1)]
        in_specs += [pl.BlockSpec((c, HEAD_DIM), ctx_k)]
        in_specs += [pl.BlockSpec(blk, nb_map(off, vcol0)) for off in (-1, 0, 1)]
        in_specs += [pl.BlockSpec((c, HEAD_DIM), ctx_v)]
        args = [z_qk, z_qk, z_qk, z_qk, z_qk, z_rest, z_rest, z_rest, z_rest]

    n_in = 1 + len(args)
    return pl.pallas_call(
        functools.partial(_attn_kernel, n_blocks=n_blocks, group=group,
                          has_local=not context_queries),
        out_shape=jax.ShapeDtypeStruct(att_prev.shape, BF16),
        grid=(b, n_blocks, N_KV_HEADS),
        in_specs=[pl.BlockSpec(memory_space=pltpu.SMEM)] + in_specs
                 + [pl.BlockSpec(memory_space=pl.ANY)],
        out_specs=pl.BlockSpec((BLOCK, qw), q_map),
        input_output_aliases={n_in: 0},
        compiler_params=_params("arbitrary", "arbitrary", "arbitrary"),
        name="attention_ctx" if context_queries else "attention",
    )(sinks, *args, att_prev)


def _softplus(x):
    return jnp.maximum(x, 0.0) + jnp.log1p(jnp.exp(-jnp.abs(x)))


def _lru_kernel(perm_ref, permt_ref,
                uf_ref, ufp_ref, ufn_ref, ub_ref, ubp_ref, ubn_ref,
                cw_ref, cb_ref, wr_ref, br_ref, wi_ref, bi_ref, lam_ref,
                hf_ref, hb_ref,
                ext_s, a_s, b_s, end_s, st_s, *, n_chunks):
    s = pl.program_id(1)
    tc, w = uf_ref.shape
    sub = tc // SUBLANES
    n_lru_blocks = w // LRU_BLOCK_DIM

    @pl.when(s == 0)
    def _():
        st_s[...] = jnp.zeros_like(st_s)

    row8 = lax.broadcasted_iota(jnp.int32, (SUBLANES, w), 0)
    is_ctx = s == 0
    for d in range(2):
        if d == 0:
            u_ref, p_ref, n_ref, o_ref = uf_ref, ufp_ref, ufn_ref, hf_ref
            at_start = is_ctx | (s == 1)
            at_end = is_ctx | (s == n_chunks)
        else:
            u_ref, p_ref, n_ref, o_ref = ub_ref, ubp_ref, ubn_ref, hb_ref
            at_start = is_ctx | (s == n_chunks)
            at_end = is_ctx | (s == 1)

        up = jnp.dot(perm_ref[...], u_ref[...], preferred_element_type=F32)
        prev = jnp.where(at_start, 0.0, p_ref[...].astype(F32))
        nxt = jnp.where(at_end, 0.0, n_ref[...].astype(F32))
        last = up[(sub - 1) * SUBLANES:]
        last2 = up[(sub - 2) * SUBLANES:(sub - 1) * SUBLANES]
        first = up[:SUBLANES]
        ext_s[0:SUBLANES] = jnp.where(row8 == 0, prev[SUBLANES - 2:SUBLANES - 1],
                                      pltpu.roll(last2, 1, 0))
        ext_s[SUBLANES:2 * SUBLANES] = jnp.where(row8 == 0, prev[SUBLANES - 1:SUBLANES],
                                                 pltpu.roll(last, 1, 0))
        ext_s[2 * SUBLANES:2 * SUBLANES + tc] = up
        ext_s[2 * SUBLANES + tc:] = jnp.where(row8 == SUBLANES - 1, nxt[0:1],
                                              pltpu.roll(first, SUBLANES - 1, 0))

        for blk in range(n_lru_blocks):
            sl = slice(blk * LRU_BLOCK_DIM, (blk + 1) * LRU_BLOCK_DIM)
            uc = cb_ref[:, sl]
            for i in range(CONV_WIDTH):
                uc = uc + ext_s[i * SUBLANES:i * SUBLANES + tc, sl] * cw_ref[i:i + 1, sl]
            ucb = uc.astype(BF16)
            r = jax.nn.sigmoid(jnp.dot(ucb, wr_ref[d, blk], preferred_element_type=F32)
                               + br_ref[d:d + 1, sl])
            gi = jax.nn.sigmoid(jnp.dot(ucb, wi_ref[d, blk], preferred_element_type=F32)
                                + bi_ref[d:d + 1, sl])
            log_a = (-LRU_C * r) * _softplus(-lam_ref[d:d + 1, sl])
            a = jnp.exp(log_a)
            mult = jnp.sqrt(1.0 - jnp.exp(2.0 * log_a))
            a_s[:, sl] = a
            b_s[:, sl] = mult * (gi * uc)

        cols = min(4 * LANES, w)
        steps = range(sub) if d == 0 else range(sub - 1, -1, -1)
        for c0 in range(0, w, cols):
            cs = slice(c0, c0 + cols)
            hloc = jnp.zeros((SUBLANES, cols), F32)
            prod = jnp.ones((SUBLANES, cols), F32)
            for t in steps:
                rows = slice(t * SUBLANES, (t + 1) * SUBLANES)
                at = a_s[rows, cs]
                hloc = at * hloc + b_s[rows, cs]
                prod = at * prod
                b_s[rows, cs] = hloc
                a_s[rows, cs] = prod
            end_s[0:SUBLANES, cs] = hloc
            end_s[SUBLANES:, cs] = prod

        hend = end_s[0:SUBLANES]
        pend = end_s[SUBLANES:]
        carry = st_s[d:d + 1]
        carries = [None] * SUBLANES
        order = range(SUBLANES) if d == 0 else range(SUBLANES - 1, -1, -1)
        for j in order:
            carries[j] = carry
            carry = hend[j:j + 1] + pend[j:j + 1] * carry
        st_s[d:d + 1] = carry
        ctile = jnp.concatenate(carries, axis=0)
        for t in range(sub):
            rows = slice(t * SUBLANES, (t + 1) * SUBLANES)
            b_s[rows] = b_s[rows] + a_s[rows] * ctile
        o_ref[...] = jnp.dot(permt_ref[...], b_s[...].astype(BF16),
                             preferred_element_type=F32).astype(o_ref.dtype)


def _lru(z_rest, u_col0, conv_w, conv_b, wr, br, wi, bi, lam, dims):
    b, s, c = dims["B"], dims["S"], dims["C"]
    r_rows = z_rest.shape[0]
    w = conv_w.shape[1]
    tc = c
    sub = tc // SUBLANES
    assert tc % (2 * SUBLANES) == 0 and s % tc == 0 and u_col0 % w == 0
    n_chunks = s // tc
    ucol = u_col0 // w
    ctx_blk0 = (b * s) // tc
    halo_per_chunk = tc // SUBLANES
    n_halo = r_rows // SUBLANES

    def fwd_blk(bi_, st):
        return jnp.where(st == 0, ctx_blk0 + bi_, bi_ * n_chunks + st - 1)

    def bwd_blk(bi_, st):
        return jnp.where(st == 0, ctx_blk0 + bi_, bi_ * n_chunks + n_chunks - st)

    def main(blk):
        return lambda bi_, st: (blk(bi_, st), ucol)

    def prev(blk):
        return lambda bi_, st: (jnp.maximum(blk(bi_, st) * halo_per_chunk - 1, 0),
                                ucol)

    def nxt(blk):
        return lambda bi_, st: (jnp.minimum((blk(bi_, st) + 1) * halo_per_chunk, n_halo - 1),
                                ucol)

    def out_map(blk):
        return lambda bi_, st: (blk(bi_, st), 0)

    def full(shape):
        return pl.BlockSpec(shape, lambda bi_, st: (0,) * len(shape))

    t_idx = jnp.arange(tc)
    src = (t_idx % SUBLANES) * sub + t_idx // SUBLANES
    perm = (src[:, None] == t_idx[None, :]).astype(BF16)
    permt = perm.T

    n_blk = w // LRU_BLOCK_DIM
    out = pl.pallas_call(
        functools.partial(_lru_kernel, n_chunks=n_chunks),
        out_shape=[jax.ShapeDtypeStruct((r_rows, w), BF16)] * 2,
        grid=(b, n_chunks + 1),
        in_specs=[full((tc, tc)), full((tc, tc)),
                  pl.BlockSpec((tc, w), main(fwd_blk)),
                  pl.BlockSpec((SUBLANES, w), prev(fwd_blk)),
                  pl.BlockSpec((SUBLANES, w), nxt(fwd_blk)),
                  pl.BlockSpec((tc, w), main(bwd_blk)),
                  pl.BlockSpec((SUBLANES, w), prev(bwd_blk)),
                  pl.BlockSpec((SUBLANES, w), nxt(bwd_blk)),
                  full((CONV_WIDTH, w)), full((1, w)),
                  full((2, n_blk, LRU_BLOCK_DIM, LRU_BLOCK_DIM)), full((2, w)),
                  full((2, n_blk, LRU_BLOCK_DIM, LRU_BLOCK_DIM)), full((2, w)),
                  full((2, w))],
        out_specs=[pl.BlockSpec((tc, w), out_map(fwd_blk)),
                   pl.BlockSpec((tc, w), out_map(bwd_blk))],
        scratch_shapes=[pltpu.VMEM((tc + 3 * SUBLANES, w), F32),
                        pltpu.VMEM((tc, w), F32),
                        pltpu.VMEM((tc, w), F32),
                        pltpu.VMEM((2 * SUBLANES, w), F32),
                        pltpu.VMEM((2, w), F32)],
        compiler_params=_params("arbitrary", "arbitrary"),
        name="conv_rglru",
    )(perm, permt, z_rest, z_rest, z_rest, z_rest, z_rest, z_rest,
      conv_w, conv_b.reshape(1, w), wr, br, wi, bi, lam)
    return out


def _gelu_tanh(x):
    return 0.5 * x * (1.0 + jnp.tanh(math.sqrt(2.0 / math.pi) * (x + 0.044715 * (x * x * x))))


def _merge_kernel(att_ref, hf_ref, hb_ref, gl_ref, woa_ref, wob_ref, ga_ref, gb_ref, o_ref, rec_s):
    @pl.when(pl.program_id(1) == 0)
    def _():
        rec = hf_ref[...].astype(F32) + hb_ref[...].astype(F32)
        rec_s[...] = (rec * _gelu_tanh(gl_ref[...].astype(F32))).astype(rec_s.dtype)

    ya = jnp.dot(att_ref[...], woa_ref[...], preferred_element_type=F32)
    yb = jnp.dot(rec_s[...], wob_ref[...], preferred_element_type=F32)
    out = jax.nn.sigmoid(ga_ref[...].astype(F32)) * ya + jax.nn.sigmoid(gb_ref[...].astype(F32)) * yb
    o_ref[...] = out.astype(o_ref.dtype)


def _merge(att, hf, hb, z_rest, cols, w_oa, w_ob, tm, n_rows):
    r, qd = att.shape
    w = hf.shape[1]
    d = w_oa.shape[1]
    tn = _tile(d, 512)
    assert cols["gl"] % w == 0 and cols["ga"] % tn == 0 and cols["gb"] % tn == 0
    gl_blk, ga_blk, gb_blk = cols["gl"] // w, cols["ga"] // tn, cols["gb"] // tn
    return pl.pallas_call(
        _merge_kernel,
        out_shape=jax.ShapeDtypeStruct((r, d), BF16),
        grid=(n_rows // tm, d // tn),
        in_specs=[pl.BlockSpec((tm, qd), lambda i, j: (i, 0)),
                  pl.BlockSpec((tm, w), lambda i, j: (i, 0)),
                  pl.BlockSpec((tm, w), lambda i, j: (i, 0)),
                  pl.BlockSpec((tm, w), lambda i, j: (i, gl_blk)),
                  pl.BlockSpec((qd, tn), lambda i, j: (0, j)),
                  pl.BlockSpec((w, tn), lambda i, j: (0, j)),
                  pl.BlockSpec((tm, tn), lambda i, j: (i, ga_blk + j)),
                  pl.BlockSpec((tm, tn), lambda i, j: (i, gb_blk + j))],
        out_specs=pl.BlockSpec((tm, tn), lambda i, j: (i, j)),
        scratch_shapes=[pltpu.VMEM((tm, w), BF16)],
        compiler_params=_params("arbitrary", "arbitrary"),
        name="gated_merge",
    )(att, hf, hb, z_rest, w_oa, w_ob, z_rest, z_rest)


def _row_copy(src_ref, src_row, dst_ref, dst_row, sem):
    return pltpu.make_async_copy(src_ref.at[pl.ds(src_row, 1)], dst_ref.at[pl.ds(dst_row, 1)], sem)


def _dispatch_kernel(pos_ref, h_ref, xs_ref, sem):
    tm = h_ref.shape[0]

    def copy(r, k):
        return _row_copy(h_ref, r, xs_ref, pos_ref[0, r * TOP_K + k], sem)

    def start(r, carry):
        for k in range(TOP_K):
            copy(r, k).start()
        return carry

    def wait(r, carry):
        for k in range(TOP_K):
            copy(r, k).wait()
        return carry

    lax.fori_loop(0, tm, start, 0)
    lax.fori_loop(0, tm, wait, 0)


def _dispatch(hp, pos3, n_slots, tm, n_rows):
    r, dh = hp.shape
    return pl.pallas_call(
        _dispatch_kernel,
        out_shape=jax.ShapeDtypeStruct((n_slots, dh), jnp.uint32),
        grid=(n_rows // tm,),
        in_specs=[pl.BlockSpec((None, 1, tm * TOP_K), lambda i: (i, 0, 0), memory_space=pltpu.SMEM),
                  pl.BlockSpec((tm, dh), lambda i: (i, 0))],
        out_specs=pl.BlockSpec(memory_space=pl.ANY),
        scratch_shapes=[pltpu.SemaphoreType.DMA(())],
        compiler_params=_params("arbitrary"),
        name="moe_dispatch",
    )(pos3, hp)


def _expert_kernel(te_ref, ti_ref, tv_ref, xs_ref, w1_ref, b1_ref, w2_ref, ys_ref, w1_s, w2_s, *,
                   expert_dim):
    t = pl.program_id(0)
    new_expert = (t == 0) | (te_ref[t] != te_ref[jnp.maximum(t - 1, 0)])

    @pl.when(new_expert)
    def _():
        w1_s[...] = w1_ref[...].astype(BF16)
        w2_s[...] = w2_ref[...].astype(BF16)

    @pl.when(tv_ref[t] > 0)
    def _():
        p = xs_ref[...]
        row = lax.broadcasted_iota(jnp.int32, p.shape, 0)
        p = jnp.where(row < tv_ref[t], p, jnp.uint32(0))
        lo, hi = _unpack_halves(p)
        xb = jnp.concatenate([lo.astype(BF16), hi.astype(BF16)], axis=1)
        hdn = jnp.dot(xb, w1_s[...], preferred_element_type=F32) + b1_ref[...]
        glu = jnp.minimum(hdn[:, :expert_dim], SWIGLU_LIMIT)
        lin = jnp.clip(hdn[:, expert_dim:], -SWIGLU_LIMIT, SWIGLU_LIMIT)
        act = glu * jax.nn.sigmoid(SWIGLU_ALPHA * glu) * (lin + 1.0)
        y = jnp.dot(act.astype(BF16), w2_s[...], preferred_element_type=F32)
        ys_ref[...] = _pack_halves(y)


def _experts(xs, tiles, w1_stack, b1_stack, w2_stack, layer, capacity, tm):
    n_slots, dh = xs.shape
    _, n_e, d, f2 = w1_stack.shape
    f = f2 // 2
    te, ti, tv = tiles
    cap_tiles = capacity // tm

    def rows(t, te_r, ti_r, tv_r):
        return (te_r[t] * cap_tiles + ti_r[t], 0)

    grid_spec = pltpu.PrefetchScalarGridSpec(
        num_scalar_prefetch=3,
        grid=(te.shape[0],),
        in_specs=[pl.BlockSpec((tm, dh), rows),
                  pl.BlockSpec((None, None, d, f2), lambda t, te_r, ti_r, tv_r: (layer, te_r[t], 0, 0)),
                  pl.BlockSpec((None, 1, f2), lambda t, te_r, ti_r, tv_r: (layer * n_e + te_r[t], 0, 0)),
                  pl.BlockSpec((None, None, f, d), lambda t, te_r, ti_r, tv_r: (layer, te_r[t], 0, 0))],
        out_specs=pl.BlockSpec((tm, dh), rows),
        scratch_shapes=[pltpu.VMEM((d, f2), BF16), pltpu.VMEM((f, d), BF16)],
    )
    return pl.pallas_call(
        functools.partial(_expert_kernel, expert_dim=f),
        out_shape=jax.ShapeDtypeStruct((n_slots, dh), jnp.uint32),
        grid_spec=grid_spec,
        compiler_params=_params("arbitrary"),
        name="moe_experts",
    )(te, ti, tv, xs, w1_stack, b1_stack.reshape(-1, 1, f2), w2_stack)


def _combine_kernel(pos_ref, wts_ref, comb_ref, b2_ref, ys_ref, o_ref, buf_s, sem):
    tm = wts_ref.shape[0]
    half = o_ref.shape[1] // 2

    def copy(r, k):
        return _row_copy(ys_ref, pos_ref[0, r * TOP_K + k], buf_s.at[k], r, sem)

    def start(r, carry):
        for k in range(TOP_K):
            copy(r, k).start()
        return carry

    def wait(r, carry):
        for k in range(TOP_K):
            copy(r, k).wait()
        return carry

    lax.fori_loop(0, tm, start, 0)
    bias = jnp.dot(comb_ref[...], b2_ref[...], precision=HIGHEST, preferred_element_type=F32)
    lax.fori_loop(0, tm, wait, 0)
    acc_lo = bias[:, :half]
    acc_hi = bias[:, half:]
    wts = wts_ref[...]
    for k in range(TOP_K):
        lo, hi = _unpack_halves(buf_s[k])
        wk = wts[:, k:k + 1]
        acc_lo = acc_lo + wk * lo
        acc_hi = acc_hi + wk * hi
    o_ref[:, :half] = acc_lo.astype(o_ref.dtype)
    o_ref[:, half:] = acc_hi.astype(o_ref.dtype)


def _combine(ys, pos3, wts, comb, b2_stack, layer, tm, n_rows, r):
    _, dh = ys.shape
    _, n_e, d = b2_stack.shape
    return pl.pallas_call(
        _combine_kernel,
        out_shape=jax.ShapeDtypeStruct((r, d), BF16),
        grid=(n_rows // tm,),
        in_specs=[pl.BlockSpec((None, 1, tm * TOP_K), lambda i: (i, 0, 0), memory_space=pltpu.SMEM),
                  pl.BlockSpec((tm, TOP_K), lambda i: (i, 0)),
                  pl.BlockSpec((tm, n_e), lambda i: (i, 0)),
                  pl.BlockSpec((None, n_e, d), lambda i: (layer, 0, 0)),
                  pl.BlockSpec(memory_space=pl.ANY)],
        out_specs=pl.BlockSpec((tm, d), lambda i: (i, 0)),
        scratch_shapes=[pltpu.VMEM((TOP_K, tm, dh), jnp.uint32), pltpu.SemaphoreType.DMA(())],
        compiler_params=_params("arbitrary"),
        name="moe_combine",
    )(pos3, wts, comb, b2_stack, ys)


def _expert_tiles(counts, tm, n_tiles):
    n_e = counts.shape[0]
    per = (counts + tm - 1) // tm
    ends = jnp.cumsum(per)
    total = ends[-1]
    t = jnp.minimum(jnp.arange(n_tiles, dtype=jnp.int32), total - 1)
    te = jnp.minimum(jnp.searchsorted(ends, t, side="right"), n_e - 1).astype(jnp.int32)
    ti = (t - (ends[te] - per[te])).astype(jnp.int32)
    tv = jnp.clip(counts[te] - ti * tm, 0, tm)
    tv = jnp.where(jnp.arange(n_tiles) < total, tv, 0).astype(jnp.int32)
    return te, ti, tv


def _rope_tables(b, s, c):
    rows = jnp.arange(s)
    pos_r = (rows // GRID_W).astype(F32)
    pos_c = (rows % GRID_W).astype(F32)
    axis_dim = HEAD_DIM // 2
    inv = ROPE_THETA ** (-jnp.arange(0, axis_dim, 2, dtype=F32) / axis_dim)
    ang_r = pos_r[:, None] * inv[None, :]
    ang_c = pos_c[:, None] * inv[None, :]
    ang = jnp.concatenate([ang_r, ang_r, ang_c, ang_c], axis=-1)
    sign = jnp.where((jnp.arange(HEAD_DIM) % (HEAD_DIM // 2)) < HEAD_DIM // 4, -1.0, 1.0)
    cos = jnp.concatenate([jnp.tile(jnp.cos(ang), (b, 1)), jnp.ones((b * c, HEAD_DIM), F32)], axis=0)
    sin = jnp.concatenate([jnp.tile(jnp.sin(ang) * sign, (b, 1)),
                           jnp.zeros((b * c, HEAD_DIM), F32)], axis=0)
    return cos, sin


def kernel(x, c, ctx, c_ctx, w_mod1, w_mod2, b_mod, g_mix, g_ffn, w_in, conv_w, conv_b, lru_wr,
           lru_br, lru_wi, lru_bi, lru_lam, sinks, w_oa, w_ob, w_out, w_router, b_router, w_exp1,
           b_exp1, w_exp2, b_exp2, g_final):
    b, s, d = x.shape
    cl = ctx.shape[1]
    depth = w_in.shape[0]
    lw = conv_w.shape[2]
    q_dim = N_Q_HEADS * HEAD_DIM
    kv_dim = N_KV_HEADS * HEAD_DIM
    n_lat = b * s
    n_all = n_lat + b * cl
    dims = {"B": b, "S": s, "C": cl}
    assert s % BLOCK == 0 and cl % BLOCK == 0 and n_lat % cl == 0

    tm = _tile(math.gcd(s, b * cl), 512)
    tm_norm = _tile(tm, 256)

    def group_of(tile):
        def f(i):
            return jnp.where(i * tile < n_lat, (i * tile) // s, b)
        return f

    g8 = -(-(b + 1) // SUBLANES) * SUBLANES
    cond = jnp.zeros((g8, d), F32).at[:b].set(c).at[b].set(c_ctx)
    xa = jnp.concatenate([x.reshape(n_lat, d), ctx.reshape(b * cl, d)], axis=0)
    cos, sin = _rope_tables(b, s, cl)

    tn = _tile(math.gcd(math.gcd(q_dim, kv_dim), math.gcd(lw, d)), 512)
    o_v = q_dim + kv_dim
    o_u = o_v + kv_dim
    n_main = (2 * lw + 2 * d) // tn
    qk_blocks = (o_v // tn, lambda j: j)
    rest_blocks = (n_main + kv_dim // tn,
                   lambda j: jnp.where(j < n_main, o_u // tn + j, o_v // tn + j - n_main))
    out_blocks = (d // tn, lambda j: j)
    rest_cols = {"u": 0, "gl": lw, "ga": 2 * lw, "gb": 2 * lw + d, "v": 2 * lw + 2 * d}

    n_experts = w_router.shape[2]
    tm_moe = tm_norm
    capacity = -(-n_all // tm_moe) * tm_moe
    n_slots = n_experts * capacity

    att = jnp.zeros((n_all, q_dim), BF16)
    delta = None
    for l in range(depth):
        last = l == depth - 1
        n_rows = n_lat if last else n_all
        mods = _modulation(cond, w_mod1[l], w_mod2[l], b_mod[l]).reshape(g8 * N_MOD, 1, d)

        if delta is None:
            xa, h = _resnorm(xa, mods, group_of(tm_norm), tm_norm, n_all, g_mix[l], 0, 1)
        else:
            xa, h = _resnorm(xa, mods, group_of(tm_norm), tm_norm, n_all, g_mix[l], 0, 1,
                             delta=delta, gate_mods=mods_prev, gate_idx=5)

        z_qk = _matmul(h, w_in, l, qk_blocks, tm, n_all, tn, rope=(cos, sin))
        z_rest = _matmul(h, w_in, l, rest_blocks, tm, n_all, tn)

        att = _attention(z_qk, z_rest, sinks[l], att, dims, rest_cols["v"], context_queries=False)
        if not last:
            att = _attention(z_qk, z_rest, sinks[l], att, dims, rest_cols["v"], context_queries=True)
        hf, hb = _lru(z_rest, rest_cols["u"], conv_w[l], conv_b[l], lru_wr[l].astype(BF16), lru_br[l],
                      lru_wi[l].astype(BF16), lru_bi[l], lru_lam[l], dims)
        merged = _merge(att, hf, hb, z_rest, rest_cols, w_oa[l].astype(BF16), w_ob[l].astype(BF16),
                        tm, n_rows)
        y = _matmul(merged, w_out, l, out_blocks, tm, n_rows, tn)
        xa, hp, comb, pos, wts, counts = _resnorm(
            xa, mods, group_of(tm_norm), tm_norm, n_rows, g_ffn[l], 3, 4, delta=y, gate_mods=mods,
            gate_idx=2, router=(w_router[l], b_router[l]), capacity=capacity)
        pos3 = pos.reshape(n_all // tm_moe, 1, tm_moe * TOP_K)
        tiles = _expert_tiles(counts[0], tm_moe, n_rows * TOP_K // tm_moe + n_experts)
        xs = _dispatch(hp, pos3, n_slots, tm_moe, n_rows)
        ys = _experts(xs, tiles, w_exp1, b_exp1, w_exp2, l, capacity, tm_moe)
        delta = _combine(ys, pos3, wts, comb, b_exp2, l, tm_moe, n_rows, n_all)
        mods_prev = mods

    out = _final_norm(xa, delta, mods_prev, group_of(tm_norm), tm_norm, n_lat, g_final, 5)
    return out.reshape(b, s, d)
```

```python
import functools
import math

import jax
import jax.numpy as jnp
from jax import lax
from jax.experimental import pallas as pl
from jax.experimental.pallas import tpu as pltpu

GRID_W = 64
N_Q_HEADS = 16
N_KV_HEADS = 4
HEAD_DIM = 128
WINDOW = 128
BLOCK = 128
ROPE_THETA = 10000.0
NEG_INF = -1e30
LRU_BLOCK_DIM = 128
CONV_WIDTH = 4
LRU_C = 8.0
TOP_K = 4
SWIGLU_LIMIT = 7.0
SWIGLU_ALPHA = 1.702
N_MOD = 6
EPS = 1e-6

V7X_VMEM_LIMIT_BYTES = 56 * 1024 * 1024
SUBLANES = 8
LANES = 128

F32 = jnp.float32
BF16 = jnp.bfloat16
HIGHEST = lax.Precision.HIGHEST


def _params(*sem):
    return pltpu.CompilerParams(dimension_semantics=sem, vmem_limit_bytes=V7X_VMEM_LIMIT_BYTES)


def _tile(n, pref):
    if n <= pref:
        return n
    t = pref
    while t >= 8:
        if n % t == 0 and t % 8 == 0:
            return t
        t -= 8
    return n


def _mod_kernel(cond_ref, w1_ref, w2_ref, b_ref, o_ref, t_ref):
    @pl.when(pl.program_id(0) == 0)
    def _():
        cnd = cond_ref[...]
        t_ref[...] = jnp.dot(cnd * jax.nn.sigmoid(cnd), w1_ref[...], precision=HIGHEST,
                             preferred_element_type=F32)

    o_ref[...] = jnp.dot(t_ref[...], w2_ref[...], precision=HIGHEST,
                         preferred_element_type=F32) + b_ref[...]


def _modulation(cond, w1, w2, b):
    g8, d = cond.shape
    mr = w1.shape[1]
    n = w2.shape[1]
    tn = _tile(n, 2048)
    return pl.pallas_call(
        _mod_kernel,
        out_shape=jax.ShapeDtypeStruct((g8, n), F32),
        grid=(n // tn,),
        in_specs=[pl.BlockSpec((g8, d), lambda j: (0, 0)),
                  pl.BlockSpec((d, mr), lambda j: (0, 0)),
                  pl.BlockSpec((mr, tn), lambda j: (0, j)),
                  pl.BlockSpec((1, tn), lambda j: (0, j))],
        out_specs=pl.BlockSpec((g8, tn), lambda j: (0, j)),
        scratch_shapes=[pltpu.VMEM((g8, mr), F32)],
        compiler_params=_params("arbitrary"),
        name="modulation",
    )(cond, w1, w2, b.reshape(1, n))


def _pack_halves(v):
    half = v.shape[1] // 2
    lo = lax.bitcast_convert_type(v[:, :half].astype(BF16).astype(F32), jnp.uint32)
    hi = lax.bitcast_convert_type(v[:, half:].astype(BF16).astype(F32), jnp.uint32)
    return (hi & jnp.uint32(0xFFFF0000)) | (lo >> jnp.uint32(16))


def _unpack_halves(p):
    lo = lax.bitcast_convert_type(p << jnp.uint32(16), F32)
    hi = lax.bitcast_convert_type(p & jnp.uint32(0xFFFF0000), F32)
    return lo, hi


def _resnorm_kernel(*refs, has_delta, with_router, n_experts, capacity):
    refs = list(refs)
    x_ref = refs.pop(0)
    if has_delta:
        d_ref = refs.pop(0)
        gate_ref = refs.pop(0)
    g_ref, sh_ref, sc_ref = refs.pop(0), refs.pop(0), refs.pop(0)
    if with_router:
        wr_ref, br_ref = refs.pop(0), refs.pop(0)
    xo_ref, h_ref = refs.pop(0), refs.pop(0)

    x = x_ref[...]
    if has_delta:
        x = x + gate_ref[...] * d_ref[...].astype(F32)
    xo_ref[...] = x
    ms = jnp.mean(x * x, axis=-1, keepdims=True)
    y = (x * lax.rsqrt(ms + EPS)) * g_ref[...]
    h = y * (1.0 + sc_ref[...]) + sh_ref[...]
    if not with_router:
        h_ref[...] = h.astype(h_ref.dtype)
        return

    h_ref[...] = _pack_halves(h)
    comb_ref, pos_ref, wts_ref, cnt_ref, run_s = refs
    tm = h.shape[0]

    @pl.when(pl.program_id(0) == 0)
    def _():
        run_s[...] = jnp.zeros_like(run_s)

    logits = jnp.dot(h, wr_ref[...], precision=HIGHEST, preferred_element_type=F32) + br_ref[...]
    lane = lax.broadcasted_iota(jnp.int32, logits.shape, 1)
    work = logits
    comb = jnp.zeros_like(logits)
    chosen = jnp.zeros_like(logits)
    denom = jnp.zeros((tm, 1), F32)
    top0 = None
    picks = []
    for k in range(TOP_K):
        m = jnp.max(work, axis=-1, keepdims=True)
        idx = jnp.min(jnp.where(work == m, lane, n_experts), axis=-1, keepdims=True)
        onehot = lane == idx
        if k == 0:
            top0 = m
        e = jnp.exp(m - top0)
        comb = comb + jnp.where(onehot, e, 0.0)
        chosen = chosen + jnp.where(onehot, 1.0, 0.0)
        denom = denom + e
        work = jnp.where(onehot, -jnp.inf, work)
        picks.append((idx, onehot, e))
    comb_ref[...] = comb / denom

    earlier = (lax.broadcasted_iota(jnp.int32, (tm, tm), 0)
               > lax.broadcasted_iota(jnp.int32, (tm, tm), 1))
    before = run_s[...] + jnp.dot(jnp.where(earlier, 1.0, 0.0).astype(BF16), chosen.astype(BF16),
                                  preferred_element_type=F32)
    lane_k = lax.broadcasted_iota(jnp.int32, (tm, TOP_K), 1)
    pos = jnp.zeros((tm, TOP_K), jnp.int32)
    wts = jnp.zeros((tm, TOP_K), F32)
    for k, (idx, onehot, e) in enumerate(picks):
        rank = jnp.sum(jnp.where(onehot, before, 0.0), axis=-1, keepdims=True).astype(jnp.int32)
        pos = jnp.where(lane_k == k, idx * capacity + rank, pos)
        wts = jnp.where(lane_k == k, e / denom, wts)
    pos_ref[...] = pos
    wts_ref[...] = wts
    run_s[...] = run_s[...] + jnp.sum(chosen, axis=0, keepdims=True)
    cnt_ref[...] = run_s[...].astype(jnp.int32)


def _resnorm(x, mods, group_of_block, tm, n_rows, g, shift_idx, scale_idx, *, delta=None,
             gate_mods=None, gate_idx=None, router=None, capacity=0):
    r, d = x.shape
    has_delta = delta is not None
    with_router = router is not None

    def row(i):
        return (i, 0)

    def mod_spec(m):
        return pl.BlockSpec((None, 1, d), lambda i: (group_of_block(i) * N_MOD + m, 0, 0))

    in_specs = [pl.BlockSpec((tm, d), row)]
    args = [x]
    if has_delta:
        in_specs += [pl.BlockSpec((tm, d), row), mod_spec(gate_idx)]
        args += [delta, gate_mods]
    in_specs += [pl.BlockSpec((1, d), lambda i: (0, 0)), mod_spec(shift_idx), mod_spec(scale_idx)]
    args += [g.reshape(1, d), mods, mods]
    out_shape = [jax.ShapeDtypeStruct((r, d), F32)]
    out_specs = [pl.BlockSpec((tm, d), row)]
    scratch = []
    n_experts = 0
    if with_router:
        w_router, b_router = router
        n_experts = w_router.shape[1]
        in_specs += [pl.BlockSpec((d, n_experts), lambda i: (0, 0)),
                     pl.BlockSpec((1, n_experts), lambda i: (0, 0))]
        args += [w_router, b_router.reshape(1, n_experts)]
        out_shape += [jax.ShapeDtypeStruct((r, d // 2), jnp.uint32),
                      jax.ShapeDtypeStruct((r, n_experts), F32),
                      jax.ShapeDtypeStruct((r, TOP_K), jnp.int32),
                      jax.ShapeDtypeStruct((r, TOP_K), F32),
                      jax.ShapeDtypeStruct((1, n_experts), jnp.int32)]
        out_specs += [pl.BlockSpec((tm, d // 2), row),
                      pl.BlockSpec((tm, n_experts), row),
                      pl.BlockSpec((tm, TOP_K), row),
                      pl.BlockSpec((tm, TOP_K), row),
                      pl.BlockSpec((1, n_experts), lambda i: (0, 0))]
        scratch = [pltpu.VMEM((1, n_experts), F32)]
    else:
        out_shape.append(jax.ShapeDtypeStruct((r, d), BF16))
        out_specs.append(pl.BlockSpec((tm, d), row))
    return pl.pallas_call(
        functools.partial(_resnorm_kernel, has_delta=has_delta, with_router=with_router,
                          n_experts=n_experts, capacity=capacity),
        out_shape=out_shape,
        grid=(n_rows // tm,),
        in_specs=in_specs,
        out_specs=out_specs,
        scratch_shapes=scratch,
        input_output_aliases={0: 0},
        compiler_params=_params("arbitrary"),
        name="resnorm_router" if with_router else "resnorm",
    )(*args)


def _final_kernel(x_ref, d_ref, gate_ref, g_ref, o_ref):
    x = x_ref[...] + gate_ref[...] * d_ref[...].astype(F32)
    ms = jnp.mean(x * x, axis=-1, keepdims=True)
    o_ref[...] = (x * lax.rsqrt(ms + EPS)) * g_ref[...]


def _final_norm(x, delta, mods, group_of_block, tm, n_rows, g, gate_idx):
    r, d = x.shape
    return pl.pallas_call(
        _final_kernel,
        out_shape=jax.ShapeDtypeStruct((n_rows, d), F32),
        grid=(n_rows // tm,),
        in_specs=[pl.BlockSpec((tm, d), lambda i: (i, 0)),
                  pl.BlockSpec((tm, d), lambda i: (i, 0)),
                  pl.BlockSpec((None, 1, d), lambda i: (group_of_block(i) * N_MOD + gate_idx, 0, 0)),
                  pl.BlockSpec((1, d), lambda i: (0, 0))],
        out_specs=pl.BlockSpec((tm, d), lambda i: (i, 0)),
        compiler_params=_params("arbitrary"),
        name="final_norm",
    )(x, delta, mods, g.reshape(1, d))


def _mm_kernel(a_ref, w_ref, o_ref, wb_s):
    @pl.when(pl.program_id(1) == 0)
    def _():
        wb_s[...] = w_ref[...].astype(BF16)

    o_ref[...] = jnp.dot(a_ref[...], wb_s[...], preferred_element_type=F32).astype(o_ref.dtype)


def _mm_rope_kernel(a_ref, w_ref, cos_ref, sin_ref, o_ref, wb_s):
    @pl.when(pl.program_id(1) == 0)
    def _():
        wb_s[...] = w_ref[...].astype(BF16)

    acc = jnp.dot(a_ref[...], wb_s[...], preferred_element_type=F32)
    cos = cos_ref[...]
    sin = sin_ref[...]
    lane = lax.broadcasted_iota(jnp.int32, cos.shape, 1)
    first_half = (lane % (HEAD_DIM // 2)) < (HEAD_DIM // 4)
    for c in range(acc.shape[1] // HEAD_DIM):
        xc = acc[:, c * HEAD_DIM:(c + 1) * HEAD_DIM]
        rot = jnp.where(first_half, pltpu.roll(xc, HEAD_DIM - HEAD_DIM // 4, 1),
                        pltpu.roll(xc, HEAD_DIM // 4, 1))
        o_ref[:, c * HEAD_DIM:(c + 1) * HEAD_DIM] = (xc * cos + rot * sin).astype(o_ref.dtype)


MATMUL_ROW_TILE_MAX = 1100


def _row_tile(n, max_rows=MATMUL_ROW_TILE_MAX, multiple=16):
    best = None
    for t in range(multiple, min(n, max_rows) + 1, multiple):
        if n % t == 0:
            best = t
    assert best is not None, n
    return best


def _matmul(a, w_stack, layer, col_blocks, n_rows, tn, rope=None):
    r, k = a.shape
    tm = _row_tile(n_rows)
    n_col = col_blocks[0]
    col_of = col_blocks[1]
    in_specs = [pl.BlockSpec((tm, k), lambda j, i: (i, 0)),
                pl.BlockSpec((None, k, tn), lambda j, i: (layer, 0, col_of(j)))]
    args = [a, w_stack]
    kern = _mm_kernel
    if rope is not None:
        in_specs += [pl.BlockSpec((tm, HEAD_DIM), lambda j, i: (i, 0)),
                     pl.BlockSpec((tm, HEAD_DIM), lambda j, i: (i, 0))]
        args += list(rope)
        kern = _mm_rope_kernel
    return pl.pallas_call(
        kern,
        out_shape=jax.ShapeDtypeStruct((r, n_col * tn), BF16),
        grid=(n_col, n_rows // tm),
        in_specs=in_specs,
        out_specs=pl.BlockSpec((tm, tn), lambda j, i: (i, j)),
        scratch_shapes=[pltpu.VMEM((k, tn), BF16)],
        compiler_params=_params("arbitrary", "arbitrary"),
        name="matmul_rope" if rope is not None else "matmul",
    )(*args)


def _attn_kernel(sink_ref, q_ref, *refs, n_blocks, group, has_local):
    n = pl.program_id(1)
    o_ref = refs[-1]
    if has_local:
        k_refs, v_refs = refs[0:4], refs[4:8]
        n_keys = 3 * BLOCK + k_refs[3].shape[0]
        t = lax.broadcasted_iota(jnp.int32, (BLOCK, n_keys), 0)
        col = lax.broadcasted_iota(jnp.int32, (BLOCK, n_keys), 1)
        rel = col - BLOCK - t
        lo = jnp.where(n > 0, 0, BLOCK)
        hi = jnp.where(n < n_blocks - 1, 3 * BLOCK, 2 * BLOCK)
        in_band = (jnp.abs(rel) <= WINDOW) & (col >= lo) & (col < hi)
        valid = jnp.where(col >= 3 * BLOCK, 1, in_band.astype(jnp.int32))
        valid = jnp.concatenate([valid] * group, axis=0) != 0
    else:
        k_refs, v_refs = refs[0:1], refs[1:2]
    for h in range(N_KV_HEADS):
        hs = slice(h * HEAD_DIM, (h + 1) * HEAD_DIM)
        q0 = h * group * HEAD_DIM
        qs = jnp.concatenate([q_ref[:, q0 + g * HEAD_DIM:q0 + (g + 1) * HEAD_DIM]
                              for g in range(group)], axis=0)
        kall = jnp.concatenate([r[:, hs] for r in k_refs], axis=0)
        vall = jnp.concatenate([r[:, hs] for r in v_refs], axis=0)
        s = lax.dot_general(qs, kall, (((1,), (1,)), ((), ())), preferred_element_type=F32)
        s = s * (HEAD_DIM ** -0.5)
        if has_local:
            s = jnp.where(valid, s, NEG_INF)
        sink = jnp.concatenate(
            [jnp.full((BLOCK, 1), sink_ref[h * group + g], F32) for g in range(group)], axis=0)
        m = jnp.maximum(jnp.max(s, axis=-1, keepdims=True), sink)
        p = jnp.exp(s - m)
        den = jnp.sum(p, axis=-1, keepdims=True) + jnp.exp(sink - m)
        o = jnp.dot(p.astype(vall.dtype), vall, preferred_element_type=F32) / den
        for g in range(group):
            o_ref[:, q0 + g * HEAD_DIM:q0 + (g + 1) * HEAD_DIM] = (
                o[g * BLOCK:(g + 1) * BLOCK].astype(o_ref.dtype))


def _attention(z_qk, z_rest, sinks, att_prev, dims, v_col0, context_queries):
    b, s, c = dims["B"], dims["S"], dims["C"]
    group = N_Q_HEADS // N_KV_HEADS
    q_dim = N_Q_HEADS * HEAD_DIM
    kv_dim = N_KV_HEADS * HEAD_DIM
    assert q_dim % kv_dim == 0 and v_col0 % kv_dim == 0
    kcol = q_dim // kv_dim
    vcol = v_col0 // kv_dim
    ctx_row0 = (b * s) // c

    if context_queries:
        n_blocks = c // BLOCK
        q_row0 = (b * s) // BLOCK

        def q_map(bi, n):
            return (q_row0 + bi * n_blocks + n, 0)

        in_specs = [pl.BlockSpec((BLOCK, q_dim), q_map),
                    pl.BlockSpec((c, kv_dim), lambda bi, n: (ctx_row0 + bi, kcol)),
                    pl.BlockSpec((c, kv_dim), lambda bi, n: (ctx_row0 + bi, vcol))]
        args = [z_qk, z_qk, z_rest]
    else:
        n_blocks = s // BLOCK

        def q_map(bi, n):
            return (bi * n_blocks + n, 0)

        def nb_map(off, col):
            return lambda bi, n: (bi * n_blocks + jnp.clip(n + off, 0, n_blocks - 1), col)

        blk = (BLOCK, kv_dim)
        in_specs = [pl.BlockSpec((BLOCK, q_dim), q_map)]
        in_specs += [pl.BlockSpec(blk, nb_map(off, kcol)) for off in (-1, 0, 1)]
        in_specs += [pl.BlockSpec((c, kv_dim), lambda bi, n: (ctx_row0 + bi, kcol))]
        in_specs += [pl.BlockSpec(blk, nb_map(off, vcol)) for off in (-1, 0, 1)]
        in_specs += [pl.BlockSpec((c, kv_dim), lambda bi, n: (ctx_row0 + bi, vcol))]
        args = [z_qk, z_qk, z_qk, z_qk, z_qk, z_rest, z_rest, z_rest, z_rest]

    in_specs = [pl.BlockSpec(memory_space=pltpu.SMEM)] + in_specs
    args = [sinks] + args
    aliases = {}
    if att_prev is not None:
        in_specs.append(pl.BlockSpec(memory_space=pl.ANY))
        aliases = {len(args): 0}
        args.append(att_prev)
    return pl.pallas_call(
        functools.partial(_attn_kernel, n_blocks=n_blocks, group=group,
                          has_local=not context_queries),
        out_shape=jax.ShapeDtypeStruct((z_qk.shape[0], q_dim), BF16),
        grid=(b, n_blocks),
        in_specs=in_specs,
        out_specs=pl.BlockSpec((BLOCK, q_dim), q_map),
        input_output_aliases=aliases,
        compiler_params=_params("arbitrary", "arbitrary"),
        name="attention_ctx" if context_queries else "attention",
    )(*args)


def _softplus(x):
    return jnp.maximum(x, 0.0) + jnp.log1p(jnp.exp(-jnp.abs(x)))


def _lru_kernel(perm_ref, permt_ref,
                uf_ref, ufp_ref, ufn_ref, ub_ref, ubp_ref, ubn_ref,
                cw_ref, cb_ref, wr_ref, br_ref, wi_ref, bi_ref, lam_ref,
                hf_ref, hb_ref,
                ext_s, a_s, b_s, end_s, st_s, *, n_chunks):
    s = pl.program_id(1)
    tc, w = uf_ref.shape
    sub = tc // SUBLANES
    n_lru_blocks = w // LRU_BLOCK_DIM

    @pl.when(s == 0)
    def _():
        st_s[...] = jnp.zeros_like(st_s)

    row8 = lax.broadcasted_iota(jnp.int32, (SUBLANES, w), 0)
    is_ctx = s == 0
    for d in range(2):
        if d == 0:
            u_ref, p_ref, n_ref, o_ref = uf_ref, ufp_ref, ufn_ref, hf_ref
            at_start = is_ctx | (s == 1)
            at_end = is_ctx | (s == n_chunks)
        else:
            u_ref, p_ref, n_ref, o_ref = ub_ref, ubp_ref, ubn_ref, hb_ref
            at_start = is_ctx | (s == n_chunks)
            at_end = is_ctx | (s == 1)

        up = jnp.dot(perm_ref[...], u_ref[...], preferred_element_type=F32)
        prev = jnp.where(at_start, 0.0, p_ref[...].astype(F32))
        nxt = jnp.where(at_end, 0.0, n_ref[...].astype(F32))
        last = up[(sub - 1) * SUBLANES:]
        last2 = up[(sub - 2) * SUBLANES:(sub - 1) * SUBLANES]
        first = up[:SUBLANES]
        ext_s[0:SUBLANES] = jnp.where(row8 == 0, prev[SUBLANES - 2:SUBLANES - 1],
                                      pltpu.roll(last2, 1, 0))
        ext_s[SUBLANES:2 * SUBLANES] = jnp.where(row8 == 0, prev[SUBLANES - 1:SUBLANES],
                                                 pltpu.roll(last, 1, 0))
        ext_s[2 * SUBLANES:2 * SUBLANES + tc] = up
        ext_s[2 * SUBLANES + tc:] = jnp.where(row8 == SUBLANES - 1, nxt[0:1],
                                              pltpu.roll(first, SUBLANES - 1, 0))

        for blk in range(n_lru_blocks):
            sl = slice(blk * LRU_BLOCK_DIM, (blk + 1) * LRU_BLOCK_DIM)
            uc = cb_ref[:, sl]
            for i in range(CONV_WIDTH):
                uc = uc + ext_s[i * SUBLANES:i * SUBLANES + tc, sl] * cw_ref[i:i + 1, sl]
            ucb = uc.astype(BF16)
            r = jax.nn.sigmoid(jnp.dot(ucb, wr_ref[d, blk], preferred_element_type=F32)
                               + br_ref[d:d + 1, sl])
            gi = jax.nn.sigmoid(jnp.dot(ucb, wi_ref[d, blk], preferred_element_type=F32)
                                + bi_ref[d:d + 1, sl])
            log_a = (-LRU_C * r) * _softplus(-lam_ref[d:d + 1, sl])
            a = jnp.exp(log_a)
            mult = jnp.sqrt(1.0 - jnp.exp(2.0 * log_a))
            a_s[:, sl] = a
            b_s[:, sl] = mult * (gi * uc)

        cols = min(4 * LANES, w)
        steps = range(sub) if d == 0 else range(sub - 1, -1, -1)
        for c0 in range(0, w, cols):
            cs = slice(c0, c0 + cols)
            hloc = jnp.zeros((SUBLANES, cols), F32)
            prod = jnp.ones((SUBLANES, cols), F32)
            for t in steps:
                rows = slice(t * SUBLANES, (t + 1) * SUBLANES)
                at = a_s[rows, cs]
                hloc = at * hloc + b_s[rows, cs]
                prod = at * prod
                b_s[rows, cs] = hloc
                a_s[rows, cs] = prod
            end_s[0:SUBLANES, cs] = hloc
            end_s[SUBLANES:, cs] = prod

        hend = end_s[0:SUBLANES]
        pend = end_s[SUBLANES:]
        carry = st_s[d:d + 1]
        carries = [None] * SUBLANES
        order = range(SUBLANES) if d == 0 else range(SUBLANES - 1, -1, -1)
        for j in order:
            carries[j] = carry
            carry = hend[j:j + 1] + pend[j:j + 1] * carry
        st_s[d:d + 1] = carry
        ctile = jnp.concatenate(carries, axis=0)
        for t in range(sub):
            rows = slice(t * SUBLANES, (t + 1) * SUBLANES)
            b_s[rows] = b_s[rows] + a_s[rows] * ctile
        o_ref[...] = jnp.dot(permt_ref[...], b_s[...].astype(BF16),
                             preferred_element_type=F32).astype(o_ref.dtype)


def _lru(z_rest, u_col0, conv_w, conv_b, wr, br, wi, bi, lam, dims):
    b, s, c = dims["B"], dims["S"], dims["C"]
    r_rows = z_rest.shape[0]
    w = conv_w.shape[1]
    tc = c
    sub = tc // SUBLANES
    assert tc % (2 * SUBLANES) == 0 and s % tc == 0 and u_col0 % w == 0
    n_chunks = s // tc
    ucol = u_col0 // w
    ctx_blk0 = (b * s) // tc
    halo_per_chunk = tc // SUBLANES
    n_halo = r_rows // SUBLANES

    def fwd_blk(bi_, st):
        return jnp.where(st == 0, ctx_blk0 + bi_, bi_ * n_chunks + st - 1)

    def bwd_blk(bi_, st):
        return jnp.where(st == 0, ctx_blk0 + bi_, bi_ * n_chunks + n_chunks - st)

    def main(blk):
        return lambda bi_, st: (blk(bi_, st), ucol)

    def prev(blk):
        return lambda bi_, st: (jnp.maximum(blk(bi_, st) * halo_per_chunk - 1, 0),
                                ucol)

    def nxt(blk):
        return lambda bi_, st: (jnp.minimum((blk(bi_, st) + 1) * halo_per_chunk, n_halo - 1),
                                ucol)

    def out_map(blk):
        return lambda bi_, st: (blk(bi_, st), 0)

    def full(shape):
        return pl.BlockSpec(shape, lambda bi_, st: (0,) * len(shape))

    t_idx = jnp.arange(tc)
    src = (t_idx % SUBLANES) * sub + t_idx // SUBLANES
    perm = (src[:, None] == t_idx[None, :]).astype(BF16)
    permt = perm.T

    n_blk = w // LRU_BLOCK_DIM
    out = pl.pallas_call(
        functools.partial(_lru_kernel, n_chunks=n_chunks),
        out_shape=[jax.ShapeDtypeStruct((r_rows, w), BF16)] * 2,
        grid=(b, n_chunks + 1),
        in_specs=[full((tc, tc)), full((tc, tc)),
                  pl.BlockSpec((tc, w), main(fwd_blk)),
                  pl.BlockSpec((SUBLANES, w), prev(fwd_blk)),
                  pl.BlockSpec((SUBLANES, w), nxt(fwd_blk)),
                  pl.BlockSpec((tc, w), main(bwd_blk)),
                  pl.BlockSpec((SUBLANES, w), prev(bwd_blk)),
                  pl.BlockSpec((SUBLANES, w), nxt(bwd_blk)),
                  full((CONV_WIDTH, w)), full((1, w)),
                  full((2, n_blk, LRU_BLOCK_DIM, LRU_BLOCK_DIM)), full((2, w)),
                  full((2, n_blk, LRU_BLOCK_DIM, LRU_BLOCK_DIM)), full((2, w)),
                  full((2, w))],
        out_specs=[pl.BlockSpec((tc, w), out_map(fwd_blk)),
                   pl.BlockSpec((tc, w), out_map(bwd_blk))],
        scratch_shapes=[pltpu.VMEM((tc + 3 * SUBLANES, w), F32),
                        pltpu.VMEM((tc, w), F32),
                        pltpu.VMEM((tc, w), F32),
                        pltpu.VMEM((2 * SUBLANES, w), F32),
                        pltpu.VMEM((2, w), F32)],
        compiler_params=_params("arbitrary", "arbitrary"),
        name="conv_rglru",
    )(perm, permt, z_rest, z_rest, z_rest, z_rest, z_rest, z_rest,
      conv_w, conv_b.reshape(1, w), wr, br, wi, bi, lam)
    return out


def _gelu_tanh(x):
    return 0.5 * x * (1.0 + jnp.tanh(math.sqrt(2.0 / math.pi) * (x + 0.044715 * (x * x * x))))


def _rec_gate_kernel(hf_ref, hb_ref, gl_ref, o_ref):
    rec = hf_ref[...].astype(F32) + hb_ref[...].astype(F32)
    o_ref[...] = (rec * _gelu_tanh(gl_ref[...].astype(F32))).astype(o_ref.dtype)


def _rec_gate(hf, hb, z_rest, gl_col0, tm, n_rows):
    r, w = hf.shape
    assert gl_col0 % w == 0
    gl_blk = gl_col0 // w
    return pl.pallas_call(
        _rec_gate_kernel,
        out_shape=jax.ShapeDtypeStruct((r, w), BF16),
        grid=(n_rows // tm,),
        in_specs=[pl.BlockSpec((tm, w), lambda i: (i, 0)),
                  pl.BlockSpec((tm, w), lambda i: (i, 0)),
                  pl.BlockSpec((tm, w), lambda i: (i, gl_blk))],
        out_specs=pl.BlockSpec((tm, w), lambda i: (i, 0)),
        compiler_params=_params("arbitrary"),
        name="rec_gate",
    )(hf, hb, z_rest)


def _merge_kernel(att_ref, rec_ref, woa_ref, wob_ref, ga_ref, gb_ref, o_ref, woa_s, wob_s):
    @pl.when(pl.program_id(1) == 0)
    def _():
        woa_s[...] = woa_ref[...].astype(BF16)
        wob_s[...] = wob_ref[...].astype(BF16)

    ya = jnp.dot(att_ref[...], woa_s[...], preferred_element_type=F32)
    yb = jnp.dot(rec_ref[...], wob_s[...], preferred_element_type=F32)
    out = jax.nn.sigmoid(ga_ref[...].astype(F32)) * ya + jax.nn.sigmoid(gb_ref[...].astype(F32)) * yb
    o_ref[...] = out.astype(o_ref.dtype)


def _merge(att, rec, z_rest, cols, w_oa_stack, w_ob_stack, layer, n_rows, tn):
    r, qd = att.shape
    w = rec.shape[1]
    d = w_oa_stack.shape[2]
    tm = _row_tile(n_rows)
    assert cols["ga"] % tn == 0 and cols["gb"] % tn == 0
    ga_blk, gb_blk = cols["ga"] // tn, cols["gb"] // tn
    return pl.pallas_call(
        _merge_kernel,
        out_shape=jax.ShapeDtypeStruct((r, d), BF16),
        grid=(d // tn, n_rows // tm),
        in_specs=[pl.BlockSpec((tm, qd), lambda j, i: (i, 0)),
                  pl.BlockSpec((tm, w), lambda j, i: (i, 0)),
                  pl.BlockSpec((None, qd, tn), lambda j, i: (layer, 0, j)),
                  pl.BlockSpec((None, w, tn), lambda j, i: (layer, 0, j)),
                  pl.BlockSpec((tm, tn), lambda j, i: (i, ga_blk + j)),
                  pl.BlockSpec((tm, tn), lambda j, i: (i, gb_blk + j))],
        out_specs=pl.BlockSpec((tm, tn), lambda j, i: (i, j)),
        scratch_shapes=[pltpu.VMEM((qd, tn), BF16), pltpu.VMEM((w, tn), BF16)],
        compiler_params=_params("arbitrary", "arbitrary"),
        name="gated_merge",
    )(att, rec, w_oa_stack, w_ob_stack, z_rest, z_rest)


def _row_copy(src_ref, src_row, dst_ref, dst_row, sem):
    return pltpu.make_async_copy(src_ref.at[pl.ds(src_row, 1)], dst_ref.at[pl.ds(dst_row, 1)], sem)


def _dispatch_kernel(pos_ref, h_ref, xs_ref, sem):
    tm = h_ref.shape[0]

    def copy(r, k):
        return _row_copy(h_ref, r, xs_ref, pos_ref[0, r * TOP_K + k], sem)

    def start(r, carry):
        for k in range(TOP_K):
            copy(r, k).start()
        return carry

    def wait(r, carry):
        for k in range(TOP_K):
            copy(r, k).wait()
        return carry

    lax.fori_loop(0, tm, start, 0)
    lax.fori_loop(0, tm, wait, 0)


def _dispatch(hp, pos3, n_slots, tm, n_rows):
    r, dh = hp.shape
    return pl.pallas_call(
        _dispatch_kernel,
        out_shape=jax.ShapeDtypeStruct((n_slots, dh), jnp.uint32),
        grid=(n_rows // tm,),
        in_specs=[pl.BlockSpec((None, 1, tm * TOP_K), lambda i: (i, 0, 0), memory_space=pltpu.SMEM),
                  pl.BlockSpec((tm, dh), lambda i: (i, 0))],
        out_specs=pl.BlockSpec(memory_space=pl.ANY),
        scratch_shapes=[pltpu.SemaphoreType.DMA(())],
        compiler_params=_params("arbitrary"),
        name="moe_dispatch",
    )(pos3, hp)


def _expert_kernel(te_ref, ti_ref, tv_ref, xs_ref, w1_ref, b1_ref, w2_ref, ys_ref, w1_s, w2_s, *,
                   expert_dim):
    t = pl.program_id(0)
    new_expert = (t == 0) | (te_ref[t] != te_ref[jnp.maximum(t - 1, 0)])

    @pl.when(new_expert)
    def _():
        w1_s[...] = w1_ref[...].astype(BF16)
        w2_s[...] = w2_ref[...].astype(BF16)

    @pl.when(tv_ref[t] > 0)
    def _():
        p = xs_ref[...]
        row = lax.broadcasted_iota(jnp.int32, p.shape, 0)
        p = jnp.where(row < tv_ref[t], p, jnp.uint32(0))
        lo, hi = _unpack_halves(p)
        xb = jnp.concatenate([lo.astype(BF16), hi.astype(BF16)], axis=1)
        hdn = jnp.dot(xb, w1_s[...], preferred_element_type=F32) + b1_ref[...]
        glu = jnp.minimum(hdn[:, :expert_dim], SWIGLU_LIMIT)
        lin = jnp.clip(hdn[:, expert_dim:], -SWIGLU_LIMIT, SWIGLU_LIMIT)
        act = glu * jax.nn.sigmoid(SWIGLU_ALPHA * glu) * (lin + 1.0)
        y = jnp.dot(act.astype(BF16), w2_s[...], preferred_element_type=F32)
        ys_ref[...] = _pack_halves(y)


def _experts(xs, tiles, w1_stack, b1_stack, w2_stack, layer, capacity, tm):
    n_slots, dh = xs.shape
    _, n_e, d, f2 = w1_stack.shape
    f = f2 // 2
    te, ti, tv = tiles
    cap_tiles = capacity // tm

    def rows(t, te_r, ti_r, tv_r):
        return (te_r[t] * cap_tiles + ti_r[t], 0)

    grid_spec = pltpu.PrefetchScalarGridSpec(
        num_scalar_prefetch=3,
        grid=(te.shape[0],),
        in_specs=[pl.BlockSpec((tm, dh), rows),
                  pl.BlockSpec((None, None, d, f2), lambda t, te_r, ti_r, tv_r: (layer, te_r[t], 0, 0)),
                  pl.BlockSpec((None, 1, f2), lambda t, te_r, ti_r, tv_r: (layer * n_e + te_r[t], 0, 0)),
                  pl.BlockSpec((None, None, f, d), lambda t, te_r, ti_r, tv_r: (layer, te_r[t], 0, 0))],
        out_specs=pl.BlockSpec((tm, dh), rows),
        scratch_shapes=[pltpu.VMEM((d, f2), BF16), pltpu.VMEM((f, d), BF16)],
    )
    return pl.pallas_call(
        functools.partial(_expert_kernel, expert_dim=f),
        out_shape=jax.ShapeDtypeStruct((n_slots, dh), jnp.uint32),
        grid_spec=grid_spec,
        compiler_params=_params("arbitrary"),
        name="moe_experts",
    )(te, ti, tv, xs, w1_stack, b1_stack.reshape(-1, 1, f2), w2_stack)


def _combine_kernel(pos_ref, wts_ref, comb_ref, b2_ref, ys_ref, o_ref, buf_s, sem):
    tm = wts_ref.shape[0]
    half = o_ref.shape[1] // 2

    def copy(r, k):
        return _row_copy(ys_ref, pos_ref[0, r * TOP_K + k], buf_s.at[k], r, sem)

    def start(r, carry):
        for k in range(TOP_K):
            copy(r, k).start()
        return carry

    def wait(r, carry):
        for k in range(TOP_K):
            copy(r, k).wait()
        return carry

    lax.fori_loop(0, tm, start, 0)
    bias = jnp.dot(comb_ref[...], b2_ref[...], precision=HIGHEST, preferred_element_type=F32)
    lax.fori_loop(0, tm, wait, 0)
    acc_lo = bias[:, :half]
    acc_hi = bias[:, half:]
    wts = wts_ref[...]
    for k in range(TOP_K):
        lo, hi = _unpack_halves(buf_s[k])
        wk = wts[:, k:k + 1]
        acc_lo = acc_lo + wk * lo
        acc_hi = acc_hi + wk * hi
    o_ref[:, :half] = acc_lo.astype(o_ref.dtype)
    o_ref[:, half:] = acc_hi.astype(o_ref.dtype)


def _combine(ys, pos3, wts, comb, b2_stack, layer, tm, n_rows, r):
    _, dh = ys.shape
    _, n_e, d = b2_stack.shape
    return pl.pallas_call(
        _combine_kernel,
        out_shape=jax.ShapeDtypeStruct((r, d), BF16),
        grid=(n_rows // tm,),
        in_specs=[pl.BlockSpec((None, 1, tm * TOP_K), lambda i: (i, 0, 0), memory_space=pltpu.SMEM),
                  pl.BlockSpec((tm, TOP_K), lambda i: (i, 0)),
                  pl.BlockSpec((tm, n_e), lambda i: (i, 0)),
                  pl.BlockSpec((None, n_e, d), lambda i: (layer, 0, 0)),
                  pl.BlockSpec(memory_space=pl.ANY)],
        out_specs=pl.BlockSpec((tm, d), lambda i: (i, 0)),
        scratch_shapes=[pltpu.VMEM((TOP_K, tm, dh), jnp.uint32), pltpu.SemaphoreType.DMA(())],
        compiler_params=_params("arbitrary"),
        name="moe_combine",
    )(pos3, wts, comb, b2_stack, ys)


def _expert_tiles(counts, tm, n_tiles):
    n_e = counts.shape[0]
    per = (counts + tm - 1) // tm
    ends = jnp.cumsum(per)
    total = ends[-1]
    t = jnp.minimum(jnp.arange(n_tiles, dtype=jnp.int32), total - 1)
    te = jnp.minimum(jnp.sum(t[:, None] >= ends[None, :], axis=1), n_e - 1).astype(jnp.int32)
    mine = jnp.arange(n_e, dtype=jnp.int32)[None, :] == te[:, None]
    ti = (t - jnp.sum(jnp.where(mine, (ends - per)[None, :], 0), axis=1)).astype(jnp.int32)
    tv = jnp.clip(jnp.sum(jnp.where(mine, counts[None, :], 0), axis=1) - ti * tm, 0, tm)
    tv = jnp.where(jnp.arange(n_tiles) < total, tv, 0).astype(jnp.int32)
    return te, ti, tv


def _rope_tables(b, s, c):
    rows = jnp.arange(s)
    pos_r = (rows // GRID_W).astype(F32)
    pos_c = (rows % GRID_W).astype(F32)
    axis_dim = HEAD_DIM // 2
    inv = ROPE_THETA ** (-jnp.arange(0, axis_dim, 2, dtype=F32) / axis_dim)
    ang_r = pos_r[:, None] * inv[None, :]
    ang_c = pos_c[:, None] * inv[None, :]
    ang = jnp.concatenate([ang_r, ang_r, ang_c, ang_c], axis=-1)
    sign = jnp.where((jnp.arange(HEAD_DIM) % (HEAD_DIM // 2)) < HEAD_DIM // 4, -1.0, 1.0)
    cos = jnp.concatenate([jnp.tile(jnp.cos(ang), (b, 1)), jnp.ones((b * c, HEAD_DIM), F32)], axis=0)
    sin = jnp.concatenate([jnp.tile(jnp.sin(ang) * sign, (b, 1)),
                           jnp.zeros((b * c, HEAD_DIM), F32)], axis=0)
    return cos, sin


def kernel(x, c, ctx, c_ctx, w_mod1, w_mod2, b_mod, g_mix, g_ffn, w_in, conv_w, conv_b, lru_wr,
           lru_br, lru_wi, lru_bi, lru_lam, sinks, w_oa, w_ob, w_out, w_router, b_router, w_exp1,
           b_exp1, w_exp2, b_exp2, g_final):
    b, s, d = x.shape
    cl = ctx.shape[1]
    depth = w_in.shape[0]
    lw = conv_w.shape[2]
    q_dim = N_Q_HEADS * HEAD_DIM
    kv_dim = N_KV_HEADS * HEAD_DIM
    n_lat = b * s
    n_all = n_lat + b * cl
    dims = {"B": b, "S": s, "C": cl}
    assert s % BLOCK == 0 and cl % BLOCK == 0 and n_lat % cl == 0

    tm = _tile(math.gcd(s, b * cl), 512)
    tm_norm = _tile(tm, 256)

    def group_of(tile):
        def f(i):
            return jnp.where(i * tile < n_lat, (i * tile) // s, b)
        return f

    g8 = -(-(b + 1) // SUBLANES) * SUBLANES
    cond = jnp.zeros((g8, d), F32).at[:b].set(c).at[b].set(c_ctx)
    xa = jnp.concatenate([x.reshape(n_lat, d), ctx.reshape(b * cl, d)], axis=0)
    cos, sin = _rope_tables(b, s, cl)

    tn = _tile(math.gcd(math.gcd(q_dim, kv_dim), math.gcd(lw, d)), 512)
    o_v = q_dim + kv_dim
    o_u = o_v + kv_dim
    n_main = (2 * lw + 2 * d) // tn
    qk_blocks = (o_v // tn, lambda j: j)
    rest_blocks = (n_main + kv_dim // tn,
                   lambda j: jnp.where(j < n_main, o_u // tn + j, o_v // tn + j - n_main))
    out_blocks = (d // tn, lambda j: j)
    rest_cols = {"u": 0, "gl": lw, "ga": 2 * lw, "gb": 2 * lw + d, "v": 2 * lw + 2 * d}

    n_experts = w_router.shape[2]
    tm_moe = tm_norm
    capacity = -(-n_all // tm_moe) * tm_moe
    n_slots = n_experts * capacity

    delta = None
    for l in range(depth):
        last = l == depth - 1
        n_rows = n_lat if last else n_all
        mods = _modulation(cond, w_mod1[l], w_mod2[l], b_mod[l]).reshape(g8 * N_MOD, 1, d)

        if delta is None:
            xa, h = _resnorm(xa, mods, group_of(tm_norm), tm_norm, n_all, g_mix[l], 0, 1)
        else:
            xa, h = _resnorm(xa, mods, group_of(tm_norm), tm_norm, n_all, g_mix[l], 0, 1,
                             delta=delta, gate_mods=mods_prev, gate_idx=5)

        z_qk = _matmul(h, w_in, l, qk_blocks, n_all, tn, rope=(cos, sin))
        z_rest = _matmul(h, w_in, l, rest_blocks, n_all, tn)

        att = _attention(z_qk, z_rest, sinks[l], None, dims, rest_cols["v"], context_queries=False)
        if not last:
            att = _attention(z_qk, z_rest, sinks[l], att, dims, rest_cols["v"], context_queries=True)
        hf, hb = _lru(z_rest, rest_cols["u"], conv_w[l], conv_b[l], lru_wr[l].astype(BF16), lru_br[l],
                      lru_wi[l].astype(BF16), lru_bi[l], lru_lam[l], dims)
        rec = _rec_gate(hf, hb, z_rest, rest_cols["gl"], tm, n_rows)
        merged = _merge(att, rec, z_rest, rest_cols, w_oa, w_ob, l, n_rows, tn)
        y = _matmul(merged, w_out, l, out_blocks, n_rows, tn)
        xa, hp, comb, pos, wts, counts = _resnorm(
            xa, mods, group_of(tm_norm), tm_norm, n_rows, g_ffn[l], 3, 4, delta=y, gate_mods=mods,
            gate_idx=2, router=(w_router[l], b_router[l]), capacity=capacity)
        pos3 = pos.reshape(n_all // tm_moe, 1, tm_moe * TOP_K)
        tiles = _expert_tiles(counts[0], tm_moe, n_rows * TOP_K // tm_moe + n_experts)
        xs = _dispatch(hp, pos3, n_slots, tm_moe, n_rows)
        ys = _experts(xs, tiles, w_exp1, b_exp1, w_exp2, l, capacity, tm_moe)
        delta = _combine(ys, pos3, wts, comb, b_exp2, l, tm_moe, n_rows, n_all)
        mods_prev = mods

    out = _final_norm(xa, delta, mods_prev, group_of(tm_norm), tm_norm, n_lat, g_final, 5)
    return out.reshape(b, s, d)
```

```python
import functools
import math

import jax
import jax.numpy as jnp
from jax import lax
from jax.experimental import pallas as pl
from jax.experimental.pallas import tpu as pltpu

GRID_W = 64
N_Q_HEADS = 16
N_KV_HEADS = 4
HEAD_DIM = 128
WINDOW = 128
BLOCK = 128
ROPE_THETA = 10000.0
NEG_INF = -1e30
LRU_BLOCK_DIM = 128
CONV_WIDTH = 4
LRU_C = 8.0
TOP_K = 4
SWIGLU_LIMIT = 7.0
SWIGLU_ALPHA = 1.702
N_MOD = 6
EPS = 1e-6

V7X_VMEM_LIMIT_BYTES = 56 * 1024 * 1024
SUBLANES = 8
LANES = 128

F32 = jnp.float32
BF16 = jnp.bfloat16
HIGHEST = lax.Precision.HIGHEST


def _params(*sem):
    return pltpu.CompilerParams(dimension_semantics=sem, vmem_limit_bytes=V7X_VMEM_LIMIT_BYTES)


def _tile(n, pref):
    if n <= pref:
        return n
    t = pref
    while t >= 8:
        if n % t == 0 and t % 8 == 0:
            return t
        t -= 8
    return n


def _mod_kernel(cond_ref, w1_ref, w2_ref, b_ref, o_ref, t_ref):
    @pl.when(pl.program_id(0) == 0)
    def _():
        cnd = cond_ref[...]
        t_ref[...] = jnp.dot(cnd * jax.nn.sigmoid(cnd), w1_ref[...], precision=HIGHEST,
                             preferred_element_type=F32)

    o_ref[...] = jnp.dot(t_ref[...], w2_ref[...], precision=HIGHEST,
                         preferred_element_type=F32) + b_ref[...]


def _modulation(cond, w1, w2, b):
    g8, d = cond.shape
    mr = w1.shape[1]
    n = w2.shape[1]
    tn = _tile(n, 2048)
    return pl.pallas_call(
        _mod_kernel,
        out_shape=jax.ShapeDtypeStruct((g8, n), F32),
        grid=(n // tn,),
        in_specs=[pl.BlockSpec((g8, d), lambda j: (0, 0)),
                  pl.BlockSpec((d, mr), lambda j: (0, 0)),
                  pl.BlockSpec((mr, tn), lambda j: (0, j)),
                  pl.BlockSpec((1, tn), lambda j: (0, j))],
        out_specs=pl.BlockSpec((g8, tn), lambda j: (0, j)),
        scratch_shapes=[pltpu.VMEM((g8, mr), F32)],
        compiler_params=_params("arbitrary"),
        name="modulation",
    )(cond, w1, w2, b.reshape(1, n))


def _pack_halves(v):
    half = v.shape[1] // 2
    lo = lax.bitcast_convert_type(v[:, :half].astype(BF16).astype(F32), jnp.uint32)
    hi = lax.bitcast_convert_type(v[:, half:].astype(BF16).astype(F32), jnp.uint32)
    return (hi & jnp.uint32(0xFFFF0000)) | (lo >> jnp.uint32(16))


def _unpack_halves(p):
    lo = lax.bitcast_convert_type(p << jnp.uint32(16), F32)
    hi = lax.bitcast_convert_type(p & jnp.uint32(0xFFFF0000), F32)
    return lo, hi


ROUTER_K_CHUNK = 1024


def _resnorm_kernel(*refs, has_delta, with_router, n_experts, capacity, lat_blocks):
    refs = list(refs)
    x_ref = refs.pop(0)
    if lat_blocks:
        xc_ref = refs.pop(0)
    if has_delta:
        d_ref = refs.pop(0)
        gate_ref = refs.pop(0)
    g_ref, sh_ref, sc_ref = refs.pop(0), refs.pop(0), refs.pop(0)
    if with_router:
        whi_ref, wlo_ref, br_ref = refs.pop(0), refs.pop(0), refs.pop(0)
    xo_ref, h_ref = refs.pop(0), refs.pop(0)

    x = x_ref[...]
    if lat_blocks:
        x = jnp.where(pl.program_id(0) < lat_blocks, x, xc_ref[...])
    if has_delta:
        x = x + gate_ref[...] * d_ref[...].astype(F32)
    xo_ref[...] = x
    ms = jnp.mean(x * x, axis=-1, keepdims=True)
    y = (x * lax.rsqrt(ms + EPS)) * g_ref[...]
    h = y * (1.0 + sc_ref[...]) + sh_ref[...]
    if not with_router:
        h_ref[...] = h.astype(h_ref.dtype)
        return

    h_ref[...] = _pack_halves(h)
    comb_ref, pos_ref, wts_ref, cnt_ref, run_s = refs
    tm = h.shape[0]

    @pl.when(pl.program_id(0) == 0)
    def _():
        run_s[...] = jnp.zeros_like(run_s)

    logits = br_ref[...]
    for c0 in range(0, h.shape[1], ROUTER_K_CHUNK):
        hc = h[:, c0:c0 + ROUTER_K_CHUNK]
        hi = hc.astype(BF16)
        lo = (hc - hi.astype(F32)).astype(BF16)
        whi = whi_ref[c0:c0 + ROUTER_K_CHUNK]
        logits = (logits + jnp.dot(hi, whi, preferred_element_type=F32)
                  + (jnp.dot(lo, whi, preferred_element_type=F32)
                     + jnp.dot(hi, wlo_ref[c0:c0 + ROUTER_K_CHUNK], preferred_element_type=F32)))
    lane = lax.broadcasted_iota(jnp.int32, logits.shape, 1)
    work = logits
    comb = jnp.zeros_like(logits)
    chosen = jnp.zeros_like(logits)
    denom = jnp.zeros((tm, 1), F32)
    top0 = None
    picks = []
    for k in range(TOP_K):
        m = jnp.max(work, axis=-1, keepdims=True)
        idx = jnp.min(jnp.where(work == m, lane, n_experts), axis=-1, keepdims=True)
        onehot = lane == idx
        if k == 0:
            top0 = m
        e = jnp.exp(m - top0)
        comb = comb + jnp.where(onehot, e, 0.0)
        chosen = chosen + jnp.where(onehot, 1.0, 0.0)
        denom = denom + e
        work = jnp.where(onehot, -jnp.inf, work)
        picks.append((idx, onehot, e))
    comb_ref[...] = comb / denom

    earlier = (lax.broadcasted_iota(jnp.int32, (tm, tm), 0)
               > lax.broadcasted_iota(jnp.int32, (tm, tm), 1))
    before = run_s[...] + jnp.dot(jnp.where(earlier, 1.0, 0.0).astype(BF16), chosen.astype(BF16),
                                  preferred_element_type=F32)
    lane_k = lax.broadcasted_iota(jnp.int32, (tm, TOP_K), 1)
    pos = jnp.zeros((tm, TOP_K), jnp.int32)
    wts = jnp.zeros((tm, TOP_K), F32)
    for k, (idx, onehot, e) in enumerate(picks):
        rank = jnp.sum(jnp.where(onehot, before, 0.0), axis=-1, keepdims=True).astype(jnp.int32)
        pos = jnp.where(lane_k == k, idx * capacity + rank, pos)
        wts = jnp.where(lane_k == k, e / denom, wts)
    pos_ref[...] = pos
    wts_ref[...] = wts
    run_s[...] = run_s[...] + jnp.sum(chosen, axis=0, keepdims=True)
    cnt_ref[...] = run_s[...].astype(jnp.int32)


def _resnorm(x, mods, group_of_block, tm, n_rows, g, shift_idx, scale_idx, *, delta=None,
             gate_mods=None, gate_idx=None, router=None, capacity=0):
    has_delta = delta is not None
    with_router = router is not None

    def row(i):
        return (i, 0)

    lat_blocks = 0
    if isinstance(x, tuple):
        x_lat, x_ctx = x
        d = x_lat.shape[1]
        r = x_lat.shape[0] + x_ctx.shape[0]
        lat_blocks = x_lat.shape[0] // tm
        in_specs = [pl.BlockSpec((tm, d), lambda i: (jnp.minimum(i, lat_blocks - 1), 0)),
                    pl.BlockSpec((tm, d), lambda i: (jnp.maximum(i - lat_blocks, 0), 0))]
        args = [x_lat, x_ctx]
    else:
        r, d = x.shape
        in_specs = [pl.BlockSpec((tm, d), row)]
        args = [x]

    def mod_spec(m):
        return pl.BlockSpec((None, 1, d), lambda i: (group_of_block(i) * N_MOD + m, 0, 0))

    if has_delta:
        in_specs += [pl.BlockSpec((tm, d), row), mod_spec(gate_idx)]
        args += [delta, gate_mods]
    in_specs += [pl.BlockSpec((1, d), lambda i: (0, 0)), mod_spec(shift_idx), mod_spec(scale_idx)]
    args += [g.reshape(1, d), mods, mods]
    out_shape = [jax.ShapeDtypeStruct((r, d), F32)]
    out_specs = [pl.BlockSpec((tm, d), row)]
    scratch = []
    n_experts = 0
    if with_router:
        w_router, b_router = router
        n_experts = w_router.shape[1]
        w_hi = w_router.astype(BF16)
        w_lo = (w_router - w_hi.astype(F32)).astype(BF16)
        in_specs += [pl.BlockSpec((d, n_experts), lambda i: (0, 0)),
                     pl.BlockSpec((d, n_experts), lambda i: (0, 0)),
                     pl.BlockSpec((1, n_experts), lambda i: (0, 0))]
        args += [w_hi, w_lo, b_router.reshape(1, n_experts)]
        out_shape += [jax.ShapeDtypeStruct((r, d // 2), jnp.uint32),
                      jax.ShapeDtypeStruct((r, n_experts), F32),
                      jax.ShapeDtypeStruct((r, TOP_K), jnp.int32),
                      jax.ShapeDtypeStruct((r, TOP_K), F32),
                      jax.ShapeDtypeStruct((1, n_experts), jnp.int32)]
        out_specs += [pl.BlockSpec((tm, d // 2), row),
                      pl.BlockSpec((tm, n_experts), row),
                      pl.BlockSpec((tm, TOP_K), row),
                      pl.BlockSpec((tm, TOP_K), row),
                      pl.BlockSpec((1, n_experts), lambda i: (0, 0))]
        scratch = [pltpu.VMEM((1, n_experts), F32)]
    else:
        out_shape.append(jax.ShapeDtypeStruct((r, d), BF16))
        out_specs.append(pl.BlockSpec((tm, d), row))
    return pl.pallas_call(
        functools.partial(_resnorm_kernel, has_delta=has_delta, with_router=with_router,
                          n_experts=n_experts, capacity=capacity, lat_blocks=lat_blocks),
        out_shape=out_shape,
        grid=(n_rows // tm,),
        in_specs=in_specs,
        out_specs=out_specs,
        scratch_shapes=scratch,
        input_output_aliases={} if lat_blocks else {0: 0},
        compiler_params=_params("arbitrary"),
        name="resnorm_router" if with_router else "resnorm",
    )(*args)


def _final_kernel(x_ref, d_ref, gate_ref, g_ref, o_ref):
    x = x_ref[...] + gate_ref[...] * d_ref[...].astype(F32)
    ms = jnp.mean(x * x, axis=-1, keepdims=True)
    o_ref[...] = (x * lax.rsqrt(ms + EPS)) * g_ref[...]


def _final_norm(x, delta, mods, group_of_block, tm, n_rows, g, gate_idx):
    r, d = x.shape
    return pl.pallas_call(
        _final_kernel,
        out_shape=jax.ShapeDtypeStruct((n_rows, d), F32),
        grid=(n_rows // tm,),
        in_specs=[pl.BlockSpec((tm, d), lambda i: (i, 0)),
                  pl.BlockSpec((tm, d), lambda i: (i, 0)),
                  pl.BlockSpec((None, 1, d), lambda i: (group_of_block(i) * N_MOD + gate_idx, 0, 0)),
                  pl.BlockSpec((1, d), lambda i: (0, 0))],
        out_specs=pl.BlockSpec((tm, d), lambda i: (i, 0)),
        compiler_params=_params("arbitrary"),
        name="final_norm",
    )(x, delta, mods, g.reshape(1, d))


def _mm_kernel(a_ref, w_ref, o_ref, wb_s):
    @pl.when(pl.program_id(1) == 0)
    def _():
        wb_s[...] = w_ref[...].astype(BF16)

    o_ref[...] = jnp.dot(a_ref[...], wb_s[...], preferred_element_type=F32).astype(o_ref.dtype)


def _mm_rope_kernel(a_ref, w_ref, cos_ref, sin_ref, o_ref, wb_s):
    @pl.when(pl.program_id(1) == 0)
    def _():
        wb_s[...] = w_ref[...].astype(BF16)

    acc = jnp.dot(a_ref[...], wb_s[...], preferred_element_type=F32)
    cos = cos_ref[...]
    sin = sin_ref[...]
    lane = lax.broadcasted_iota(jnp.int32, cos.shape, 1)
    first_half = (lane % (HEAD_DIM // 2)) < (HEAD_DIM // 4)
    for c in range(acc.shape[1] // HEAD_DIM):
        xc = acc[:, c * HEAD_DIM:(c + 1) * HEAD_DIM]
        rot = jnp.where(first_half, pltpu.roll(xc, HEAD_DIM - HEAD_DIM // 4, 1),
                        pltpu.roll(xc, HEAD_DIM // 4, 1))
        o_ref[:, c * HEAD_DIM:(c + 1) * HEAD_DIM] = (xc * cos + rot * sin).astype(o_ref.dtype)


MATMUL_ROW_TILE_MAX = 1100


def _row_tile(n, max_rows=MATMUL_ROW_TILE_MAX, multiple=16):
    best = None
    for t in range(multiple, min(n, max_rows) + 1, multiple):
        if n % t == 0:
            best = t
    assert best is not None, n
    return best


def _matmul(a, w_stack, layer, col_blocks, n_rows, tn, rope=None):
    r, k = a.shape
    tm = _row_tile(n_rows)
    n_col = col_blocks[0]
    col_of = col_blocks[1]
    in_specs = [pl.BlockSpec((tm, k), lambda j, i: (i, 0)),
                pl.BlockSpec((None, k, tn), lambda j, i: (layer, 0, col_of(j)))]
    args = [a, w_stack]
    kern = _mm_kernel
    if rope is not None:
        in_specs += [pl.BlockSpec((tm, HEAD_DIM), lambda j, i: (i, 0)),
                     pl.BlockSpec((tm, HEAD_DIM), lambda j, i: (i, 0))]
        args += list(rope)
        kern = _mm_rope_kernel
    return pl.pallas_call(
        kern,
        out_shape=jax.ShapeDtypeStruct((r, n_col * tn), BF16),
        grid=(n_col, n_rows // tm),
        in_specs=in_specs,
        out_specs=pl.BlockSpec((tm, tn), lambda j, i: (i, j)),
        scratch_shapes=[pltpu.VMEM((k, tn), BF16)],
        compiler_params=_params("arbitrary", "arbitrary"),
        name="matmul_rope" if rope is not None else "matmul",
    )(*args)


def _attn_kernel(sink_ref, q_ref, *refs, n_blocks, group, has_local):
    n = pl.program_id(1)
    o_ref = refs[-1]
    if has_local:
        k_refs, v_refs = refs[0:4], refs[4:8]
        n_keys = 3 * BLOCK + k_refs[3].shape[0]
        t = lax.broadcasted_iota(jnp.int32, (BLOCK, n_keys), 0)
        col = lax.broadcasted_iota(jnp.int32, (BLOCK, n_keys), 1)
        rel = col - BLOCK - t
        lo = jnp.where(n > 0, 0, BLOCK)
        hi = jnp.where(n < n_blocks - 1, 3 * BLOCK, 2 * BLOCK)
        in_band = (jnp.abs(rel) <= WINDOW) & (col >= lo) & (col < hi)
        valid = jnp.where(col >= 3 * BLOCK, 1, in_band.astype(jnp.int32))
        valid = jnp.concatenate([valid] * group, axis=0) != 0
    else:
        k_refs, v_refs = refs[0:1], refs[1:2]
    for h in range(N_KV_HEADS):
        hs = slice(h * HEAD_DIM, (h + 1) * HEAD_DIM)
        q0 = h * group * HEAD_DIM
        qs = jnp.concatenate([q_ref[:, q0 + g * HEAD_DIM:q0 + (g + 1) * HEAD_DIM]
                              for g in range(group)], axis=0)
        kall = jnp.concatenate([r[:, hs] for r in k_refs], axis=0)
        vall = jnp.concatenate([r[:, hs] for r in v_refs], axis=0)
        s = lax.dot_general(qs, kall, (((1,), (1,)), ((), ())), preferred_element_type=F32)
        s = s * (HEAD_DIM ** -0.5)
        if has_local:
            s = jnp.where(valid, s, NEG_INF)
        sink = jnp.concatenate(
            [jnp.full((BLOCK, 1), sink_ref[h * group + g], F32) for g in range(group)], axis=0)
        m = jnp.maximum(jnp.max(s, axis=-1, keepdims=True), sink)
        p = jnp.exp(s - m)
        den = jnp.sum(p, axis=-1, keepdims=True) + jnp.exp(sink - m)
        o = jnp.dot(p.astype(vall.dtype), vall, preferred_element_type=F32) / den
        for g in range(group):
            o_ref[:, q0 + g * HEAD_DIM:q0 + (g + 1) * HEAD_DIM] = (
                o[g * BLOCK:(g + 1) * BLOCK].astype(o_ref.dtype))


def _attention(z_qk, z_rest, sinks, att_prev, dims, v_col0, context_queries):
    b, s, c = dims["B"], dims["S"], dims["C"]
    group = N_Q_HEADS // N_KV_HEADS
    q_dim = N_Q_HEADS * HEAD_DIM
    kv_dim = N_KV_HEADS * HEAD_DIM
    assert q_dim % kv_dim == 0 and v_col0 % kv_dim == 0
    kcol = q_dim // kv_dim
    vcol = v_col0 // kv_dim
    ctx_row0 = (b * s) // c

    if context_queries:
        n_blocks = c // BLOCK
        q_row0 = (b * s) // BLOCK

        def q_map(bi, n):
            return (q_row0 + bi * n_blocks + n, 0)

        in_specs = [pl.BlockSpec((BLOCK, q_dim), q_map),
                    pl.BlockSpec((c, kv_dim), lambda bi, n: (ctx_row0 + bi, kcol)),
                    pl.BlockSpec((c, kv_dim), lambda bi, n: (ctx_row0 + bi, vcol))]
        args = [z_qk, z_qk, z_rest]
    else:
        n_blocks = s // BLOCK

        def q_map(bi, n):
            return (bi * n_blocks + n, 0)

        def nb_map(off, col):
            return lambda bi, n: (bi * n_blocks + jnp.clip(n + off, 0, n_blocks - 1), col)

        blk = (BLOCK, kv_dim)
        in_specs = [pl.BlockSpec((BLOCK, q_dim), q_map)]
        in_specs += [pl.BlockSpec(blk, nb_map(off, kcol)) for off in (-1, 0, 1)]
        in_specs += [pl.BlockSpec((c, kv_dim), lambda bi, n: (ctx_row0 + bi, kcol))]
        in_specs += [pl.BlockSpec(blk, nb_map(off, vcol)) for off in (-1, 0, 1)]
        in_specs += [pl.BlockSpec((c, kv_dim), lambda bi, n: (ctx_row0 + bi, vcol))]
        args = [z_qk, z_qk, z_qk, z_qk, z_qk, z_rest, z_rest, z_rest, z_rest]

    in_specs = [pl.BlockSpec(memory_space=pltpu.SMEM)] + in_specs
    args = [sinks] + args
    aliases = {}
    if att_prev is not None:
        in_specs.append(pl.BlockSpec(memory_space=pl.ANY))
        aliases = {len(args): 0}
        args.append(att_prev)
    return pl.pallas_call(
        functools.partial(_attn_kernel, n_blocks=n_blocks, group=group,
                          has_local=not context_queries),
        out_shape=jax.ShapeDtypeStruct((z_qk.shape[0], q_dim), BF16),
        grid=(b, n_blocks),
        in_specs=in_specs,
        out_specs=pl.BlockSpec((BLOCK, q_dim), q_map),
        input_output_aliases=aliases,
        compiler_params=_params("arbitrary", "arbitrary"),
        name="attention_ctx" if context_queries else "attention",
    )(*args)


def _softplus(x):
    return jnp.maximum(x, 0.0) + jnp.log1p(jnp.exp(-jnp.abs(x)))


def _lru_kernel(perm_ref, permt_ref,
                uf_ref, ufp_ref, ufn_ref, ub_ref, ubp_ref, ubn_ref,
                cw_ref, cb_ref, wr_ref, br_ref, wi_ref, bi_ref, lam_ref,
                hf_ref, hb_ref,
                ext_s, a_s, b_s, end_s, st_s, *, n_chunks):
    s = pl.program_id(1)
    tc, w = uf_ref.shape
    sub = tc // SUBLANES
    n_lru_blocks = w // LRU_BLOCK_DIM

    @pl.when(s == 0)
    def _():
        st_s[...] = jnp.zeros_like(st_s)

    row8 = lax.broadcasted_iota(jnp.int32, (SUBLANES, w), 0)
    is_ctx = s == 0
    for d in range(2):
        if d == 0:
            u_ref, p_ref, n_ref, o_ref = uf_ref, ufp_ref, ufn_ref, hf_ref
            at_start = is_ctx | (s == 1)
            at_end = is_ctx | (s == n_chunks)
        else:
            u_ref, p_ref, n_ref, o_ref = ub_ref, ubp_ref, ubn_ref, hb_ref
            at_start = is_ctx | (s == n_chunks)
            at_end = is_ctx | (s == 1)

        up = jnp.dot(perm_ref[...], u_ref[...], preferred_element_type=F32)
        prev = jnp.where(at_start, 0.0, p_ref[...].astype(F32))
        nxt = jnp.where(at_end, 0.0, n_ref[...].astype(F32))
        last = up[(sub - 1) * SUBLANES:]
        last2 = up[(sub - 2) * SUBLANES:(sub - 1) * SUBLANES]
        first = up[:SUBLANES]
        ext_s[0:SUBLANES] = jnp.where(row8 == 0, prev[SUBLANES - 2:SUBLANES - 1],
                                      pltpu.roll(last2, 1, 0))
        ext_s[SUBLANES:2 * SUBLANES] = jnp.where(row8 == 0, prev[SUBLANES - 1:SUBLANES],
                                                 pltpu.roll(last, 1, 0))
        ext_s[2 * SUBLANES:2 * SUBLANES + tc] = up
        ext_s[2 * SUBLANES + tc:] = jnp.where(row8 == SUBLANES - 1, nxt[0:1],
                                              pltpu.roll(first, SUBLANES - 1, 0))

        for blk in range(n_lru_blocks):
            sl = slice(blk * LRU_BLOCK_DIM, (blk + 1) * LRU_BLOCK_DIM)
            uc = cb_ref[:, sl]
            for i in range(CONV_WIDTH):
                uc = uc + ext_s[i * SUBLANES:i * SUBLANES + tc, sl] * cw_ref[i:i + 1, sl]
            ucb = uc.astype(BF16)
            r = jax.nn.sigmoid(jnp.dot(ucb, wr_ref[d, blk], preferred_element_type=F32)
                               + br_ref[d:d + 1, sl])
            gi = jax.nn.sigmoid(jnp.dot(ucb, wi_ref[d, blk], preferred_element_type=F32)
                                + bi_ref[d:d + 1, sl])
            log_a = (-LRU_C * r) * _softplus(-lam_ref[d:d + 1, sl])
            a = jnp.exp(log_a)
            mult = jnp.sqrt(1.0 - jnp.exp(2.0 * log_a))
            a_s[:, sl] = a
            b_s[:, sl] = mult * (gi * uc)

        cols = min(4 * LANES, w)
        steps = range(sub) if d == 0 else range(sub - 1, -1, -1)
        for c0 in range(0, w, cols):
            cs = slice(c0, c0 + cols)
            hloc = jnp.zeros((SUBLANES, cols), F32)
            prod = jnp.ones((SUBLANES, cols), F32)
            for t in steps:
                rows = slice(t * SUBLANES, (t + 1) * SUBLANES)
                at = a_s[rows, cs]
                hloc = at * hloc + b_s[rows, cs]
                prod = at * prod
                b_s[rows, cs] = hloc
                a_s[rows, cs] = prod
            end_s[0:SUBLANES, cs] = hloc
            end_s[SUBLANES:, cs] = prod

        hend = end_s[0:SUBLANES]
        pend = end_s[SUBLANES:]
        carry = st_s[d:d + 1]
        carries = [None] * SUBLANES
        order = range(SUBLANES) if d == 0 else range(SUBLANES - 1, -1, -1)
        for j in order:
            carries[j] = carry
            carry = hend[j:j + 1] + pend[j:j + 1] * carry
        st_s[d:d + 1] = carry
        ctile = jnp.concatenate(carries, axis=0)
        for t in range(sub):
            rows = slice(t * SUBLANES, (t + 1) * SUBLANES)
            b_s[rows] = b_s[rows] + a_s[rows] * ctile
        o_ref[...] = jnp.dot(permt_ref[...], b_s[...].astype(BF16),
                             preferred_element_type=F32).astype(o_ref.dtype)


def _lru(z_rest, u_col0, conv_w, conv_b, wr, br, wi, bi, lam, dims):
    b, s, c = dims["B"], dims["S"], dims["C"]
    r_rows = z_rest.shape[0]
    w = conv_w.shape[1]
    tc = c
    sub = tc // SUBLANES
    assert tc % (2 * SUBLANES) == 0 and s % tc == 0 and u_col0 % w == 0
    n_chunks = s // tc
    ucol = u_col0 // w
    ctx_blk0 = (b * s) // tc
    halo_per_chunk = tc // SUBLANES
    n_halo = r_rows // SUBLANES

    def fwd_blk(bi_, st):
        return jnp.where(st == 0, ctx_blk0 + bi_, bi_ * n_chunks + st - 1)

    def bwd_blk(bi_, st):
        return jnp.where(st == 0, ctx_blk0 + bi_, bi_ * n_chunks + n_chunks - st)

    def main(blk):
        return lambda bi_, st: (blk(bi_, st), ucol)

    def prev(blk):
        return lambda bi_, st: (jnp.maximum(blk(bi_, st) * halo_per_chunk - 1, 0),
                                ucol)

    def nxt(blk):
        return lambda bi_, st: (jnp.minimum((blk(bi_, st) + 1) * halo_per_chunk, n_halo - 1),
                                ucol)

    def out_map(blk):
        return lambda bi_, st: (blk(bi_, st), 0)

    def full(shape):
        return pl.BlockSpec(shape, lambda bi_, st: (0,) * len(shape))

    t_idx = jnp.arange(tc)
    src = (t_idx % SUBLANES) * sub + t_idx // SUBLANES
    perm = (src[:, None] == t_idx[None, :]).astype(BF16)
    permt = perm.T

    n_blk = w // LRU_BLOCK_DIM
    out = pl.pallas_call(
        functools.partial(_lru_kernel, n_chunks=n_chunks),
        out_shape=[jax.ShapeDtypeStruct((r_rows, w), BF16)] * 2,
        grid=(b, n_chunks + 1),
        in_specs=[full((tc, tc)), full((tc, tc)),
                  pl.BlockSpec((tc, w), main(fwd_blk)),
                  pl.BlockSpec((SUBLANES, w), prev(fwd_blk)),
                  pl.BlockSpec((SUBLANES, w), nxt(fwd_blk)),
                  pl.BlockSpec((tc, w), main(bwd_blk)),
                  pl.BlockSpec((SUBLANES, w), prev(bwd_blk)),
                  pl.BlockSpec((SUBLANES, w), nxt(bwd_blk)),
                  full((CONV_WIDTH, w)), full((1, w)),
                  full((2, n_blk, LRU_BLOCK_DIM, LRU_BLOCK_DIM)), full((2, w)),
                  full((2, n_blk, LRU_BLOCK_DIM, LRU_BLOCK_DIM)), full((2, w)),
                  full((2, w))],
        out_specs=[pl.BlockSpec((tc, w), out_map(fwd_blk)),
                   pl.BlockSpec((tc, w), out_map(bwd_blk))],
        scratch_shapes=[pltpu.VMEM((tc + 3 * SUBLANES, w), F32),
                        pltpu.VMEM((tc, w), F32),
                        pltpu.VMEM((tc, w), F32),
                        pltpu.VMEM((2 * SUBLANES, w), F32),
                        pltpu.VMEM((2, w), F32)],
        compiler_params=_params("arbitrary", "arbitrary"),
        name="conv_rglru",
    )(perm, permt, z_rest, z_rest, z_rest, z_rest, z_rest, z_rest,
      conv_w, conv_b.reshape(1, w), wr, br, wi, bi, lam)
    return out


def _gelu_tanh(x):
    return 0.5 * x * (1.0 + jnp.tanh(math.sqrt(2.0 / math.pi) * (x + 0.044715 * (x * x * x))))


def _rec_gate_kernel(hf_ref, hb_ref, gl_ref, o_ref):
    rec = hf_ref[...].astype(F32) + hb_ref[...].astype(F32)
    o_ref[...] = (rec * _gelu_tanh(gl_ref[...].astype(F32))).astype(o_ref.dtype)


def _rec_gate(hf, hb, z_rest, gl_col0, tm, n_rows):
    r, w = hf.shape
    assert gl_col0 % w == 0
    gl_blk = gl_col0 // w
    return pl.pallas_call(
        _rec_gate_kernel,
        out_shape=jax.ShapeDtypeStruct((r, w), BF16),
        grid=(n_rows // tm,),
        in_specs=[pl.BlockSpec((tm, w), lambda i: (i, 0)),
                  pl.BlockSpec((tm, w), lambda i: (i, 0)),
                  pl.BlockSpec((tm, w), lambda i: (i, gl_blk))],
        out_specs=pl.BlockSpec((tm, w), lambda i: (i, 0)),
        compiler_params=_params("arbitrary"),
        name="rec_gate",
    )(hf, hb, z_rest)


def _merge_kernel(att_ref, rec_ref, woa_ref, wob_ref, ga_ref, gb_ref, o_ref, woa_s, wob_s):
    @pl.when(pl.program_id(1) == 0)
    def _():
        woa_s[...] = woa_ref[...].astype(BF16)
        wob_s[...] = wob_ref[...].astype(BF16)

    ya = jnp.dot(att_ref[...], woa_s[...], preferred_element_type=F32)
    yb = jnp.dot(rec_ref[...], wob_s[...], preferred_element_type=F32)
    out = jax.nn.sigmoid(ga_ref[...].astype(F32)) * ya + jax.nn.sigmoid(gb_ref[...].astype(F32)) * yb
    o_ref[...] = out.astype(o_ref.dtype)


def _merge(att, rec, z_rest, cols, w_oa_stack, w_ob_stack, layer, n_rows, tn):
    r, qd = att.shape
    w = rec.shape[1]
    d = w_oa_stack.shape[2]
    tm = _row_tile(n_rows)
    assert cols["ga"] % tn == 0 and cols["gb"] % tn == 0
    ga_blk, gb_blk = cols["ga"] // tn, cols["gb"] // tn
    return pl.pallas_call(
        _merge_kernel,
        out_shape=jax.ShapeDtypeStruct((r, d), BF16),
        grid=(d // tn, n_rows // tm),
        in_specs=[pl.BlockSpec((tm, qd), lambda j, i: (i, 0)),
                  pl.BlockSpec((tm, w), lambda j, i: (i, 0)),
                  pl.BlockSpec((None, qd, tn), lambda j, i: (layer, 0, j)),
                  pl.BlockSpec((None, w, tn), lambda j, i: (layer, 0, j)),
                  pl.BlockSpec((tm, tn), lambda j, i: (i, ga_blk + j)),
                  pl.BlockSpec((tm, tn), lambda j, i: (i, gb_blk + j))],
        out_specs=pl.BlockSpec((tm, tn), lambda j, i: (i, j)),
        scratch_shapes=[pltpu.VMEM((qd, tn), BF16), pltpu.VMEM((w, tn), BF16)],
        compiler_params=_params("arbitrary", "arbitrary"),
        name="gated_merge",
    )(att, rec, w_oa_stack, w_ob_stack, z_rest, z_rest)


DMA_PRIORITIES = 2


def _row_copy(src_ref, src_row, dst_ref, dst_row, sem):
    return pltpu.make_async_copy(src_ref.at[pl.ds(src_row, 1)], dst_ref.at[pl.ds(dst_row, 1)], sem)


def _dispatch_kernel(pos_ref, h_ref, xs_ref, sem):
    tm = h_ref.shape[0]

    def copy(r, k):
        return _row_copy(h_ref, r, xs_ref, pos_ref[0, r * TOP_K + k], sem)

    def start(r, carry):
        for k in range(TOP_K):
            copy(r, k).start(priority=k % DMA_PRIORITIES)
        return carry

    def wait(r, carry):
        for k in range(TOP_K):
            copy(r, k).wait()
        return carry

    lax.fori_loop(0, tm, start, 0)
    lax.fori_loop(0, tm, wait, 0)


def _dispatch(hp, pos3, n_slots, tm, n_rows):
    r, dh = hp.shape
    return pl.pallas_call(
        _dispatch_kernel,
        out_shape=jax.ShapeDtypeStruct((n_slots, dh), jnp.uint32),
        grid=(n_rows // tm,),
        in_specs=[pl.BlockSpec((None, 1, tm * TOP_K), lambda i: (i, 0, 0), memory_space=pltpu.SMEM),
                  pl.BlockSpec((tm, dh), lambda i: (i, 0))],
        out_specs=pl.BlockSpec(memory_space=pl.ANY),
        scratch_shapes=[pltpu.SemaphoreType.DMA(())],
        compiler_params=_params("arbitrary"),
        name="moe_dispatch",
    )(pos3, hp)


def _expert_kernel(te_ref, ti_ref, tv_ref, xs_ref, w1_ref, b1_ref, w2_ref, ys_ref, w1_s, w2_s, *,
                   expert_dim):
    t = pl.program_id(0)
    new_expert = (t == 0) | (te_ref[t] != te_ref[jnp.maximum(t - 1, 0)])

    @pl.when(new_expert)
    def _():
        w1_s[...] = w1_ref[...].astype(BF16)
        w2_s[...] = w2_ref[...].astype(BF16)

    @pl.when(tv_ref[t] > 0)
    def _():
        p = xs_ref[...]
        row = lax.broadcasted_iota(jnp.int32, p.shape, 0)
        p = jnp.where(row < tv_ref[t], p, jnp.uint32(0))
        lo, hi = _unpack_halves(p)
        xb = jnp.concatenate([lo.astype(BF16), hi.astype(BF16)], axis=1)
        hdn = jnp.dot(xb, w1_s[...], preferred_element_type=F32) + b1_ref[...]
        glu = jnp.minimum(hdn[:, :expert_dim], SWIGLU_LIMIT)
        lin = jnp.clip(hdn[:, expert_dim:], -SWIGLU_LIMIT, SWIGLU_LIMIT)
        act = glu * jax.nn.sigmoid(SWIGLU_ALPHA * glu) * (lin + 1.0)
        y = jnp.dot(act.astype(BF16), w2_s[...], preferred_element_type=F32)
        ys_ref[...] = _pack_halves(y)


def _experts(xs, tiles, w1_stack, b1_stack, w2_stack, layer, capacity, tm):
    n_slots, dh = xs.shape
    _, n_e, d, f2 = w1_stack.shape
    f = f2 // 2
    te, ti, tv = tiles
    cap_tiles = capacity // tm

    def rows(t, te_r, ti_r, tv_r):
        return (te_r[t] * cap_tiles + ti_r[t], 0)

    grid_spec = pltpu.PrefetchScalarGridSpec(
        num_scalar_prefetch=3,
        grid=(te.shape[0],),
        in_specs=[pl.BlockSpec((tm, dh), rows),
                  pl.BlockSpec((None, None, d, f2), lambda t, te_r, ti_r, tv_r: (layer, te_r[t], 0, 0)),
                  pl.BlockSpec((None, 1, f2), lambda t, te_r, ti_r, tv_r: (layer * n_e + te_r[t], 0, 0)),
                  pl.BlockSpec((None, None, f, d), lambda t, te_r, ti_r, tv_r: (layer, te_r[t], 0, 0))],
        out_specs=pl.BlockSpec((tm, dh), rows),
        scratch_shapes=[pltpu.VMEM((d, f2), BF16), pltpu.VMEM((f, d), BF16)],
    )
    return pl.pallas_call(
        functools.partial(_expert_kernel, expert_dim=f),
        out_shape=jax.ShapeDtypeStruct((n_slots, dh), jnp.uint32),
        grid_spec=grid_spec,
        compiler_params=_params("arbitrary"),
        name="moe_experts",
    )(te, ti, tv, xs, w1_stack, b1_stack.reshape(-1, 1, f2), w2_stack)


def _combine_kernel(pos_ref, wts_ref, comb_ref, b2_ref, ys_ref, o_ref, buf_s, sem):
    tm = wts_ref.shape[0]
    half = o_ref.shape[1] // 2

    def copy(r, k):
        return _row_copy(ys_ref, pos_ref[0, r * TOP_K + k], buf_s.at[k], r, sem)

    def start(r, carry):
        for k in range(TOP_K):
            copy(r, k).start(priority=k % DMA_PRIORITIES)
        return carry

    def wait(r, carry):
        for k in range(TOP_K):
            copy(r, k).wait()
        return carry

    lax.fori_loop(0, tm, start, 0)
    bias =jnp.dot(comb_ref[...], b2_ref[...], precision=HIGHEST, preferred_element_type=F32)
    lax.fori_loop(0, tm, wait, 0)
    acc_lo = bias[:, :half]
    acc_hi = bias[:, half:]
    wts = wts_ref[...]
    for k in range(TOP_K):
        lo, hi = _unpack_halves(buf_s[k])
        wk = wts[:, k:k + 1]
        acc_lo = acc_lo + wk * lo
        acc_hi = acc_hi + wk * hi
    o_ref[:, :half] = acc_lo.astype(o_ref.dtype)
    o_ref[:, half:] = acc_hi.astype(o_ref.dtype)


def _combine(ys, pos3, wts, comb, b2_stack, layer, tm, n_rows, r):
    _, dh = ys.shape
    _, n_e, d = b2_stack.shape
    return pl.pallas_call(
        _combine_kernel,
        out_shape=jax.ShapeDtypeStruct((r, d), BF16),
        grid=(n_rows // tm,),
        in_specs=[pl.BlockSpec((None, 1, tm * TOP_K), lambda i: (i, 0, 0), memory_space=pltpu.SMEM),
                  pl.BlockSpec((tm, TOP_K), lambda i: (i, 0)),
                  pl.BlockSpec((tm, n_e), lambda i: (i, 0)),
                  pl.BlockSpec((None, n_e, d), lambda i: (layer, 0, 0)),
                  pl.BlockSpec(memory_space=pl.ANY)],
        out_specs=pl.BlockSpec((tm, d), lambda i: (i, 0)),
        scratch_shapes=[pltpu.VMEM((TOP_K, tm, dh), jnp.uint32), pltpu.SemaphoreType.DMA(())],
        compiler_params=_params("arbitrary"),
        name="moe_combine",
    )(pos3, wts, comb, b2_stack, ys)


def _expert_tiles(counts, tm, n_tiles):
    n_e = counts.shape[0]
    per = (counts + tm - 1) // tm
    ends = jnp.cumsum(per)
    total = ends[-1]
    t = jnp.minimum(jnp.arange(n_tiles, dtype=jnp.int32), total - 1)
    te = jnp.minimum(jnp.sum(t[:, None] >= ends[None, :], axis=1), n_e - 1).astype(jnp.int32)
    mine = jnp.arange(n_e, dtype=jnp.int32)[None, :] == te[:, None]
    ti = (t - jnp.sum(jnp.where(mine, (ends - per)[None, :], 0), axis=1)).astype(jnp.int32)
    tv = jnp.clip(jnp.sum(jnp.where(mine, counts[None, :], 0), axis=1) - ti * tm, 0, tm)
    tv = jnp.where(jnp.arange(n_tiles) < total, tv, 0).astype(jnp.int32)
    return te, ti, tv


def _rope_tables(b, s, c):
    rows = jnp.arange(s)
    pos_r = (rows // GRID_W).astype(F32)
    pos_c = (rows % GRID_W).astype(F32)
    axis_dim = HEAD_DIM // 2
    inv = ROPE_THETA ** (-jnp.arange(0, axis_dim, 2, dtype=F32) / axis_dim)
    ang_r = pos_r[:, None] * inv[None, :]
    ang_c = pos_c[:, None] * inv[None, :]
    ang = jnp.concatenate([ang_r, ang_r, ang_c, ang_c], axis=-1)
    sign = jnp.where((jnp.arange(HEAD_DIM) % (HEAD_DIM // 2)) < HEAD_DIM // 4, -1.0, 1.0)
    cos = jnp.concatenate([jnp.tile(jnp.cos(ang), (b, 1)), jnp.ones((b * c, HEAD_DIM), F32)], axis=0)
    sin = jnp.concatenate([jnp.tile(jnp.sin(ang) * sign, (b, 1)),
                           jnp.zeros((b * c, HEAD_DIM), F32)], axis=0)
    return cos, sin


def kernel(x, c, ctx, c_ctx, w_mod1, w_mod2, b_mod, g_mix, g_ffn, w_in, conv_w, conv_b, lru_wr,
           lru_br, lru_wi, lru_bi, lru_lam, sinks, w_oa, w_ob, w_out, w_router, b_router, w_exp1,
           b_exp1, w_exp2, b_exp2, g_final):
    b, s, d = x.shape
    cl = ctx.shape[1]
    depth = w_in.shape[0]
    lw = conv_w.shape[2]
    q_dim = N_Q_HEADS * HEAD_DIM
    kv_dim = N_KV_HEADS * HEAD_DIM
    n_lat = b * s
    n_all = n_lat + b * cl
    dims = {"B": b, "S": s, "C": cl}
    assert s % BLOCK == 0 and cl % BLOCK == 0 and n_lat % cl == 0

    tm = _tile(math.gcd(s, b * cl), 512)
    tm_norm = _tile(tm, 256)

    def group_of(tile):
        def f(i):
            return jnp.where(i * tile < n_lat, (i * tile) // s, b)
        return f

    g8 = -(-(b + 1) // SUBLANES) * SUBLANES
    cond = jnp.zeros((g8, d), F32).at[:b].set(c).at[b].set(c_ctx)
    xa = (x.reshape(n_lat, d), ctx.reshape(b * cl, d))
    cos, sin = _rope_tables(b, s, cl)

    tn = _tile(math.gcd(math.gcd(q_dim, kv_dim), math.gcd(lw, d)), 512)
    o_v = q_dim + kv_dim
    o_u = o_v + kv_dim
    n_main = (2 * lw + 2 * d) // tn
    qk_blocks = (o_v // tn, lambda j: j)
    rest_blocks = (n_main + kv_dim // tn,
                   lambda j: jnp.where(j < n_main, o_u // tn + j, o_v // tn + j - n_main))
    out_blocks = (d // tn, lambda j: j)
    rest_cols = {"u": 0, "gl": lw, "ga": 2 * lw, "gb": 2 * lw + d, "v": 2 * lw + 2 * d}

    n_experts = w_router.shape[2]
    tm_moe = tm_norm
    capacity = -(-n_all // tm_moe) * tm_moe
    n_slots = n_experts * capacity

    delta = None
    for l in range(depth):
        last = l == depth - 1
        n_rows = n_lat if last else n_all
        mods = _modulation(cond, w_mod1[l], w_mod2[l], b_mod[l]).reshape(g8 * N_MOD, 1, d)

        if delta is None:
            xa, h = _resnorm(xa, mods, group_of(tm_norm), tm_norm, n_all, g_mix[l], 0, 1)
        else:
            xa, h = _resnorm(xa, mods, group_of(tm_norm), tm_norm, n_all, g_mix[l], 0, 1,
                             delta=delta, gate_mods=mods_prev, gate_idx=5)

        z_qk = _matmul(h, w_in, l, qk_blocks, n_all, tn, rope=(cos, sin))
        z_rest = _matmul(h, w_in, l, rest_blocks, n_all, tn)

        att = _attention(z_qk, z_rest, sinks[l], None, dims, rest_cols["v"], context_queries=False)
        if not last:
            att = _attention(z_qk, z_rest, sinks[l], att, dims, rest_cols["v"], context_queries=True)
        hf, hb = _lru(z_rest, rest_cols["u"], conv_w[l], conv_b[l], lru_wr[l].astype(BF16), lru_br[l],
                      lru_wi[l].astype(BF16), lru_bi[l], lru_lam[l], dims)
        rec = _rec_gate(hf, hb, z_rest, rest_cols["gl"], tm, n_rows)
        merged = _merge(att, rec, z_rest, rest_cols, w_oa, w_ob, l, n_rows, tn)
        y = _matmul(merged, w_out, l, out_blocks, n_rows, tn)
        xa, hp, comb, pos, wts, counts = _resnorm(
            xa, mods, group_of(tm_norm), tm_norm, n_rows, g_ffn[l], 3, 4, delta=y, gate_mods=mods,
            gate_idx=2, router=(w_router[l], b_router[l]), capacity=capacity)
        pos3 = pos.reshape(n_all // tm_moe, 1, tm_moe * TOP_K)
        tiles = _expert_tiles(counts[0], tm_moe, n_rows * TOP_K // tm_moe + n_experts)
        xs = _dispatch(hp, pos3, n_slots, tm_moe, n_rows)
        ys = _experts(xs, tiles, w_exp1, b_exp1, w_exp2, l, capacity, tm_moe)
        delta = _combine(ys, pos3, wts, comb, b_exp2, l, tm_moe, n_rows, n_all)
        mods_prev = mods

    out = _final_norm(xa, delta, mods_prev, group_of(tm_norm), tm_norm, n_lat, g_final, 5)
    return out.reshape(b, s, d)
```

```python
import functools
import math

import jax
import jax.numpy as jnp
from jax import lax
from jax.experimental import pallas as pl
from jax.experimental.pallas import tpu as pltpu

GRID_W = 64
N_Q_HEADS = 16
N_KV_HEADS = 4
HEAD_DIM = 128
WINDOW = 128
BLOCK = 128
ROPE_THETA = 10000.0
NEG_INF = -1e30
LRU_BLOCK_DIM = 128
CONV_WIDTH = 4
LRU_C = 8.0
TOP_K = 4
SWIGLU_LIMIT = 7.0
SWIGLU_ALPHA = 1.702
N_MOD = 6
EPS = 1e-6

V7X_VMEM_LIMIT_BYTES = 56 * 1024 * 1024
SUBLANES = 8
LANES = 128

F32 = jnp.float32
BF16 = jnp.bfloat16
HIGHEST = lax.Precision.HIGHEST


def _params(*sem):
    return pltpu.CompilerParams(dimension_semantics=sem, vmem_limit_bytes=V7X_VMEM_LIMIT_BYTES)


def _tile(n, pref):
    if n <= pref:
        return n
    t = pref
    while t >= 8:
        if n % t == 0 and t % 8 == 0:
            return t
        t -= 8
    return n


def _mod_kernel(cond_ref, w1_ref, w2_ref, b_ref, o_ref, t_ref):
    @pl.when(pl.program_id(0) == 0)
    def _():
        cnd = cond_ref[...]
        t_ref[...] = jnp.dot(cnd * jax.nn.sigmoid(cnd), w1_ref[...], precision=HIGHEST,
                             preferred_element_type=F32)

    o_ref[...] = jnp.dot(t_ref[...], w2_ref[...], precision=HIGHEST,
                         preferred_element_type=F32) + b_ref[...]


def _modulation(cond, w1, w2, b):
    g8, d = cond.shape
    mr = w1.shape[1]
    n = w2.shape[1]
    tn = _tile(n, 2048)
    return pl.pallas_call(
        _mod_kernel,
        out_shape=jax.ShapeDtypeStruct((g8, n), F32),
        grid=(n // tn,),
        in_specs=[pl.BlockSpec((g8, d), lambda j: (0, 0)),
                  pl.BlockSpec((d, mr), lambda j: (0, 0)),
                  pl.BlockSpec((mr, tn), lambda j: (0, j)),
                  pl.BlockSpec((1, tn), lambda j: (0, j))],
        out_specs=pl.BlockSpec((g8, tn), lambda j: (0, j)),
        scratch_shapes=[pltpu.VMEM((g8, mr), F32)],
        compiler_params=_params("arbitrary"),
        name="modulation",
    )(cond, w1, w2, b.reshape(1, n))


def _pack_halves(v):
    half = v.shape[1] // 2

    def rounded(part):
        bits = lax.bitcast_convert_type(part, jnp.uint32)
        return bits + jnp.uint32(0x7FFF) + ((bits >> jnp.uint32(16)) & jnp.uint32(1))

    return (rounded(v[:, half:]) & jnp.uint32(0xFFFF0000)) | (rounded(v[:, :half]) >> jnp.uint32(16))


def _unpack_halves(p):
    lo = lax.bitcast_convert_type(p << jnp.uint32(16), F32)
    hi = lax.bitcast_convert_type(p & jnp.uint32(0xFFFF0000), F32)
    return lo, hi


def _bf16_part(v):
    bits = lax.bitcast_convert_type(v, jnp.uint32) & jnp.uint32(0xFFFF0000)
    return lax.bitcast_convert_type(bits, F32)


ROUTER_K_CHUNK = 1024


def _resnorm_kernel(*refs, has_delta, with_router, n_experts, capacity, lat_blocks):
    refs = list(refs)
    x_ref = refs.pop(0)
    if lat_blocks:
        xc_ref = refs.pop(0)
    if has_delta:
        d_ref = refs.pop(0)
        gate_ref = refs.pop(0)
    g_ref, sh_ref, sc_ref = refs.pop(0), refs.pop(0), refs.pop(0)
    if with_router:
        whi_ref, wlo_ref, br_ref = refs.pop(0), refs.pop(0), refs.pop(0)
    xo_ref, h_ref = refs.pop(0), refs.pop(0)

    x = x_ref[...]
    if lat_blocks:
        x = jnp.where(pl.program_id(0) < lat_blocks, x, xc_ref[...])
    if has_delta:
        x = x + gate_ref[...] * d_ref[...].astype(F32)
    xo_ref[...] = x
    ms = jnp.mean(x * x, axis=-1, keepdims=True)
    y = (x * lax.rsqrt(ms + EPS)) * g_ref[...]
    h = y * (1.0 + sc_ref[...]) + sh_ref[...]
    if not with_router:
        h_ref[...] = h.astype(h_ref.dtype)
        return

    h_ref[...] = _pack_halves(h)
    comb_ref, pos_ref, wts_ref, cnt_ref, run_s = refs
    tm = h.shape[0]

    @pl.when(pl.program_id(0) == 0)
    def _():
        run_s[...] = jnp.zeros_like(run_s)

    logits = br_ref[...]
    for c0 in range(0, h.shape[1], ROUTER_K_CHUNK):
        hc = h[:, c0:c0 + ROUTER_K_CHUNK]
        hi_f32 = _bf16_part(hc)
        hi = hi_f32.astype(BF16)
        lo = (hc - hi_f32).astype(BF16)
        whi = whi_ref[c0:c0 + ROUTER_K_CHUNK]
        logits = (logits + jnp.dot(hi, whi, preferred_element_type=F32)
                  + (jnp.dot(lo, whi, preferred_element_type=F32)
                     + jnp.dot(hi, wlo_ref[c0:c0 + ROUTER_K_CHUNK], preferred_element_type=F32)))
    lane = lax.broadcasted_iota(jnp.int32, logits.shape, 1)
    work = logits
    comb = jnp.zeros_like(logits)
    chosen = jnp.zeros_like(logits)
    denom = jnp.zeros((tm, 1), F32)
    top0 = None
    picks = []
    for k in range(TOP_K):
        m = jnp.max(work, axis=-1, keepdims=True)
        idx = jnp.min(jnp.where(work == m, lane, n_experts), axis=-1, keepdims=True)
        onehot = lane == idx
        if k == 0:
            top0 = m
        e = jnp.exp(m - top0)
        comb = comb + jnp.where(onehot, e, 0.0)
        chosen = chosen + jnp.where(onehot, 1.0, 0.0)
        denom = denom + e
        work = jnp.where(onehot, -jnp.inf, work)
        picks.append((idx, onehot, e))
    comb_ref[...] = comb / denom

    earlier = (lax.broadcasted_iota(jnp.int32, (tm, tm), 0)
               > lax.broadcasted_iota(jnp.int32, (tm, tm), 1))
    before = run_s[...] + jnp.dot(jnp.where(earlier, 1.0, 0.0).astype(BF16), chosen.astype(BF16),
                                  preferred_element_type=F32)
    lane_k = lax.broadcasted_iota(jnp.int32, (tm, TOP_K), 1)
    pos = jnp.zeros((tm, TOP_K), jnp.int32)
    wts = jnp.zeros((tm, TOP_K), F32)
    for k, (idx, onehot, e) in enumerate(picks):
        rank = jnp.sum(jnp.where(onehot, before, 0.0), axis=-1, keepdims=True).astype(jnp.int32)
        pos = jnp.where(lane_k == k, idx * capacity + rank, pos)
        wts = jnp.where(lane_k == k, e / denom, wts)
    pos_ref[...] = pos
    wts_ref[...] = wts
    run_s[...] = run_s[...] + jnp.sum(chosen, axis=0, keepdims=True)
    cnt_ref[...] = run_s[...].astype(jnp.int32)


def _resnorm(x, mods, group_of_block, tm, n_rows, g, shift_idx, scale_idx, *, delta=None,
             gate_mods=None, gate_idx=None, router=None, capacity=0):
    has_delta = delta is not None
    with_router = router is not None

    def row(i):
        return (i, 0)

    lat_blocks = 0
    if isinstance(x, tuple):
        x_lat, x_ctx = x
        d = x_lat.shape[1]
        r = x_lat.shape[0] + x_ctx.shape[0]
        lat_blocks = x_lat.shape[0] // tm
        in_specs = [pl.BlockSpec((tm, d), lambda i: (jnp.minimum(i, lat_blocks - 1), 0)),
                    pl.BlockSpec((tm, d), lambda i: (jnp.maximum(i - lat_blocks, 0), 0))]
        args = [x_lat, x_ctx]
    else:
        r, d = x.shape
        in_specs = [pl.BlockSpec((tm, d), row)]
        args = [x]

    def mod_spec(m):
        return pl.BlockSpec((None, 1, d), lambda i: (group_of_block(i) * N_MOD + m, 0, 0))

    if has_delta:
        in_specs += [pl.BlockSpec((tm, d), row), mod_spec(gate_idx)]
        args += [delta, gate_mods]
    in_specs += [pl.BlockSpec((1, d), lambda i: (0, 0)), mod_spec(shift_idx), mod_spec(scale_idx)]
    args += [g.reshape(1, d), mods, mods]
    out_shape = [jax.ShapeDtypeStruct((r, d), F32)]
    out_specs = [pl.BlockSpec((tm, d), row)]
    scratch = []
    n_experts = 0
    if with_router:
        w_router, b_router = router
        n_experts = w_router.shape[1]
        w_hi_f32 = _bf16_part(w_router)
        w_hi = w_hi_f32.astype(BF16)
        w_lo = (w_router - w_hi_f32).astype(BF16)
        in_specs += [pl.BlockSpec((d, n_experts), lambda i: (0, 0)),
                     pl.BlockSpec((d, n_experts), lambda i: (0, 0)),
                     pl.BlockSpec((1, n_experts), lambda i: (0, 0))]
        args += [w_hi, w_lo, b_router.reshape(1, n_experts)]
        out_shape += [jax.ShapeDtypeStruct((r, d // 2), jnp.uint32),
                      jax.ShapeDtypeStruct((r, n_experts), F32),
                      jax.ShapeDtypeStruct((r, TOP_K), jnp.int32),
                      jax.ShapeDtypeStruct((r, TOP_K), F32),
                      jax.ShapeDtypeStruct((1, n_experts), jnp.int32)]
        out_specs += [pl.BlockSpec((tm, d // 2), row),
                      pl.BlockSpec((tm, n_experts), row),
                      pl.BlockSpec((tm, TOP_K), row),
                      pl.BlockSpec((tm, TOP_K), row),
                      pl.BlockSpec((1, n_experts), lambda i: (0, 0))]
        scratch = [pltpu.VMEM((1, n_experts), F32)]
    else:
        out_shape.append(jax.ShapeDtypeStruct((r, d), BF16))
        out_specs.append(pl.BlockSpec((tm, d), row))
    return pl.pallas_call(
        functools.partial(_resnorm_kernel, has_delta=has_delta, with_router=with_router,
                          n_experts=n_experts, capacity=capacity, lat_blocks=lat_blocks),
        out_shape=out_shape,
        grid=(n_rows // tm,),
        in_specs=in_specs,
        out_specs=out_specs,
        scratch_shapes=scratch,
        input_output_aliases={} if lat_blocks else {0: 0},
        compiler_params=_params("arbitrary"),
        name="resnorm_router" if with_router else "resnorm",
    )(*args)


def _mm_kernel(a_ref, w_ref, o_ref, wb_s):
    @pl.when(pl.program_id(1) == 0)
    def _():
        wb_s[...] = w_ref[...].astype(BF16)

    o_ref[...] = jnp.dot(a_ref[...], wb_s[...], preferred_element_type=F32).astype(o_ref.dtype)


def _mm_rope_kernel(a_ref, w_ref, cos_ref, sin_ref, o_ref, wb_s):
    @pl.when(pl.program_id(1) == 0)
    def _():
        wb_s[...] = w_ref[...].astype(BF16)

    acc = jnp.dot(a_ref[...], wb_s[...], preferred_element_type=F32)
    cos = cos_ref[...]
    sin = sin_ref[...]
    lane = lax.broadcasted_iota(jnp.int32, cos.shape, 1)
    first_half = (lane % (HEAD_DIM // 2)) < (HEAD_DIM // 4)
    for c in range(acc.shape[1] // HEAD_DIM):
        xc = acc[:, c * HEAD_DIM:(c + 1) * HEAD_DIM]
        rot = jnp.where(first_half, pltpu.roll(xc, HEAD_DIM - HEAD_DIM // 4, 1),
                        pltpu.roll(xc, HEAD_DIM // 4, 1))
        o_ref[:, c * HEAD_DIM:(c + 1) * HEAD_DIM] = (xc * cos + rot * sin).astype(o_ref.dtype)


MATMUL_ROW_TILE_MAX = 1100


def _row_tile(n, max_rows=MATMUL_ROW_TILE_MAX, multiple=16):
    best = None
    for t in range(multiple, min(n, max_rows) + 1, multiple):
        if n % t == 0:
            best = t
    assert best is not None, n
    return best


def _matmul(a, w_stack, layer, col_blocks, n_rows, tn, rope=None):
    r, k = a.shape
    tm = _row_tile(n_rows)
    n_col = col_blocks[0]
    col_of = col_blocks[1]
    in_specs = [pl.BlockSpec((tm, k), lambda j, i: (i, 0)),
                pl.BlockSpec((None, k, tn), lambda j, i: (layer, 0, col_of(j)))]
    args = [a, w_stack]
    kern = _mm_kernel
    if rope is not None:
        in_specs += [pl.BlockSpec((tm, HEAD_DIM), lambda j, i: (i, 0)),
                     pl.BlockSpec((tm, HEAD_DIM), lambda j, i: (i, 0))]
        args += list(rope)
        kern = _mm_rope_kernel
    return pl.pallas_call(
        kern,
        out_shape=jax.ShapeDtypeStruct((r, n_col * tn), BF16),
        grid=(n_col, n_rows // tm),
        in_specs=in_specs,
        out_specs=pl.BlockSpec((tm, tn), lambda j, i: (i, j)),
        scratch_shapes=[pltpu.VMEM((k, tn), BF16)],
        compiler_params=_params("arbitrary", "arbitrary"),
        name="matmul_rope" if rope is not None else "matmul",
    )(*args)


def _attn_kernel(sink_ref, q_ref, *refs, n_blocks, group, has_local):
    n = pl.program_id(1)
    o_ref = refs[-1]
    if has_local:
        k_refs, v_refs = refs[0:4], refs[4:8]
        n_keys = 3 * BLOCK + k_refs[3].shape[0]
        t = lax.broadcasted_iota(jnp.int32, (BLOCK, n_keys), 0)
        col = lax.broadcasted_iota(jnp.int32, (BLOCK, n_keys), 1)
        rel = col - BLOCK - t
        lo = jnp.where(n > 0, 0, BLOCK)
        hi = jnp.where(n < n_blocks - 1, 3 * BLOCK, 2 * BLOCK)
        in_band = (jnp.abs(rel) <= WINDOW) & (col >= lo) & (col < hi)
        valid = jnp.where(col >= 3 * BLOCK, 1, in_band.astype(jnp.int32))
        valid = jnp.concatenate([valid] * group, axis=0) != 0
    else:
        k_refs, v_refs = refs[0:1], refs[1:2]
    for h in range(N_KV_HEADS):
        hs = slice(h * HEAD_DIM, (h + 1) * HEAD_DIM)
        q0 = h * group * HEAD_DIM
        qs = jnp.concatenate([q_ref[:, q0 + g * HEAD_DIM:q0 + (g + 1) * HEAD_DIM]
                              for g in range(group)], axis=0)
        kall = jnp.concatenate([r[:, hs] for r in k_refs], axis=0)
        vall = jnp.concatenate([r[:, hs] for r in v_refs], axis=0)
        s = lax.dot_general(qs, kall, (((1,), (1,)), ((), ())), preferred_element_type=F32)
        s = s * (HEAD_DIM ** -0.5)
        if has_local:
            s = jnp.where(valid, s, NEG_INF)
        sink = jnp.concatenate(
            [jnp.full((BLOCK, 1), sink_ref[h * group + g], F32) for g in range(group)], axis=0)
        m = jnp.maximum(jnp.max(s, axis=-1, keepdims=True), sink)
        p = jnp.exp(s - m)
        den = jnp.sum(p, axis=-1, keepdims=True) + jnp.exp(sink - m)
        o = jnp.dot(p.astype(vall.dtype), vall, preferred_element_type=F32) / den
        for g in range(group):
            o_ref[:, q0 + g * HEAD_DIM:q0 + (g + 1) * HEAD_DIM] = (
                o[g * BLOCK:(g + 1) * BLOCK].astype(o_ref.dtype))


def _attention(z_qk, z_rest, sinks, att_prev, dims, v_col0, context_queries):
    b, s, c = dims["B"], dims["S"], dims["C"]
    group = N_Q_HEADS // N_KV_HEADS
    q_dim = N_Q_HEADS * HEAD_DIM
    kv_dim = N_KV_HEADS * HEAD_DIM
    assert q_dim % kv_dim == 0 and v_col0 % kv_dim == 0
    kcol = q_dim // kv_dim
    vcol = v_col0 // kv_dim
    ctx_row0 = (b * s) // c

    if context_queries:
        n_blocks = c // BLOCK
        q_row0 = (b * s) // BLOCK

        def q_map(bi, n):
            return (q_row0 + bi * n_blocks + n, 0)

        in_specs = [pl.BlockSpec((BLOCK, q_dim), q_map),
                    pl.BlockSpec((c, kv_dim), lambda bi, n: (ctx_row0 + bi, kcol)),
                    pl.BlockSpec((c, kv_dim), lambda bi, n: (ctx_row0 + bi, vcol))]
        args = [z_qk, z_qk, z_rest]
    else:
        n_blocks = s // BLOCK

        def q_map(bi, n):
            return (bi * n_blocks + n, 0)

        def nb_map(off, col):
            return lambda bi, n: (bi * n_blocks + jnp.clip(n + off, 0, n_blocks - 1), col)

        blk = (BLOCK, kv_dim)
        in_specs = [pl.BlockSpec((BLOCK, q_dim), q_map)]
        in_specs += [pl.BlockSpec(blk, nb_map(off, kcol)) for off in (-1, 0, 1)]
        in_specs += [pl.BlockSpec((c, kv_dim), lambda bi, n: (ctx_row0 + bi, kcol))]
        in_specs += [pl.BlockSpec(blk, nb_map(off, vcol)) for off in (-1, 0, 1)]
        in_specs += [pl.BlockSpec((c, kv_dim), lambda bi, n: (ctx_row0 + bi, vcol))]
        args = [z_qk, z_qk, z_qk, z_qk, z_qk, z_rest, z_rest, z_rest, z_rest]

    in_specs = [pl.BlockSpec(memory_space=pltpu.SMEM)] + in_specs
    args = [sinks] + args
    aliases = {}
    if att_prev is not None:
        in_specs.append(pl.BlockSpec(memory_space=pl.ANY))
        aliases = {len(args): 0}
        args.append(att_prev)
    return pl.pallas_call(
        functools.partial(_attn_kernel, n_blocks=n_blocks, group=group,
                          has_local=not context_queries),
        out_shape=jax.ShapeDtypeStruct((z_qk.shape[0], q_dim), BF16),
        grid=(b, n_blocks),
        in_specs=in_specs,
        out_specs=pl.BlockSpec((BLOCK, q_dim), q_map),
        input_output_aliases=aliases,
        compiler_params=_params("arbitrary", "arbitrary"),
        name="attention_ctx" if context_queries else "attention",
    )(*args)


def _softplus(x):
    return jnp.maximum(x, 0.0) + jnp.log1p(jnp.exp(-jnp.abs(x)))


def _lru_kernel(perm_ref, permt_ref,
                uf_ref, ufp_ref, ufn_ref, ub_ref, ubp_ref, ubn_ref,
                cw_ref, cb_ref, wr_ref, br_ref, wi_ref, bi_ref, lam_ref,
                hf_ref, hb_ref,
                ext_s, a_s, b_s, end_s, st_s, *, n_chunks):
    s = pl.program_id(1)
    tc, w = uf_ref.shape
    sub = tc // SUBLANES
    n_lru_blocks = w // LRU_BLOCK_DIM

    @pl.when(s == 0)
    def _():
        st_s[...] = jnp.zeros_like(st_s)

    row8 = lax.broadcasted_iota(jnp.int32, (SUBLANES, w), 0)
    is_ctx = s == 0
    for d in range(2):
        if d == 0:
            u_ref, p_ref, n_ref, o_ref = uf_ref, ufp_ref, ufn_ref, hf_ref
            at_start = is_ctx | (s == 1)
            at_end = is_ctx | (s == n_chunks)
        else:
            u_ref, p_ref, n_ref, o_ref = ub_ref, ubp_ref, ubn_ref, hb_ref
            at_start = is_ctx | (s == n_chunks)
            at_end = is_ctx | (s == 1)

        up = jnp.dot(perm_ref[...], u_ref[...], preferred_element_type=F32)
        prev = jnp.where(at_start, 0.0, p_ref[...].astype(F32))
        nxt = jnp.where(at_end, 0.0, n_ref[...].astype(F32))
        last = up[(sub - 1) * SUBLANES:]
        last2 = up[(sub - 2) * SUBLANES:(sub - 1) * SUBLANES]
        first = up[:SUBLANES]
        ext_s[0:SUBLANES] = jnp.where(row8 == 0, prev[SUBLANES - 2:SUBLANES - 1],
                                      pltpu.roll(last2, 1, 0))
        ext_s[SUBLANES:2 * SUBLANES] = jnp.where(row8 == 0, prev[SUBLANES - 1:SUBLANES],
                                                 pltpu.roll(last, 1, 0))
        ext_s[2 * SUBLANES:2 * SUBLANES + tc] = up
        ext_s[2 * SUBLANES + tc:] = jnp.where(row8 == SUBLANES - 1, nxt[0:1],
                                              pltpu.roll(first, SUBLANES - 1, 0))

        for blk in range(n_lru_blocks):
            sl = slice(blk * LRU_BLOCK_DIM, (blk + 1) * LRU_BLOCK_DIM)
            uc = cb_ref[:, sl]
            for i in range(CONV_WIDTH):
                uc = uc + ext_s[i * SUBLANES:i * SUBLANES + tc, sl] * cw_ref[i:i + 1, sl]
            ucb = uc.astype(BF16)
            r = jax.nn.sigmoid(jnp.dot(ucb, wr_ref[d, blk], preferred_element_type=F32)
                               + br_ref[d:d + 1, sl])
            gi = jax.nn.sigmoid(jnp.dot(ucb, wi_ref[d, blk], preferred_element_type=F32)
                                + bi_ref[d:d + 1, sl])
            log_a = (-LRU_C * r) * _softplus(-lam_ref[d:d + 1, sl])
            a = jnp.exp(log_a)
            mult = jnp.sqrt(1.0 - jnp.exp(2.0 * log_a))
            a_s[:, sl] = a
            b_s[:, sl] = mult * (gi * uc)

        cols = min(4 * LANES, w)
        steps = range(sub) if d == 0 else range(sub - 1, -1, -1)
        for c0 in range(0, w, cols):
            cs = slice(c0, c0 + cols)
            hloc = jnp.zeros((SUBLANES, cols), F32)
            prod = jnp.ones((SUBLANES, cols), F32)
            for t in steps:
                rows = slice(t * SUBLANES, (t + 1) * SUBLANES)
                at = a_s[rows, cs]
                hloc = at * hloc + b_s[rows, cs]
                prod = at * prod
                b_s[rows, cs] = hloc
                a_s[rows, cs] = prod
            end_s[0:SUBLANES, cs] = hloc
            end_s[SUBLANES:, cs] = prod

        hend = end_s[0:SUBLANES]
        pend = end_s[SUBLANES:]
        carry = st_s[d:d + 1]
        carries = [None] * SUBLANES
        order = range(SUBLANES) if d == 0 else range(SUBLANES - 1, -1, -1)
        for j in order:
            carries[j] = carry
            carry = hend[j:j + 1] + pend[j:j + 1] * carry
        st_s[d:d + 1] = carry
        ctile = jnp.concatenate(carries, axis=0)
        for t in range(sub):
            rows = slice(t * SUBLANES, (t + 1) * SUBLANES)
            b_s[rows] = b_s[rows] + a_s[rows] * ctile
        o_ref[...] = jnp.dot(permt_ref[...], b_s[...].astype(BF16),
                             preferred_element_type=F32).astype(o_ref.dtype)


def _lru(z_rest, u_col0, conv_w, conv_b, wr, br, wi, bi, lam, dims):
    b, s, c = dims["B"], dims["S"], dims["C"]
    r_rows = z_rest.shape[0]
    w = conv_w.shape[1]
    tc = c
    sub = tc // SUBLANES
    assert tc % (2 * SUBLANES) == 0 and s % tc == 0 and u_col0 % w == 0
    n_chunks = s // tc
    ucol = u_col0 // w
    ctx_blk0 = (b * s) // tc
    halo_per_chunk = tc // SUBLANES
    n_halo = r_rows // SUBLANES

    def fwd_blk(bi_, st):
        return jnp.where(st == 0, ctx_blk0 + bi_, bi_ * n_chunks + st - 1)

    def bwd_blk(bi_, st):
        return jnp.where(st == 0, ctx_blk0 + bi_, bi_ * n_chunks + n_chunks - st)

    def main(blk):
        return lambda bi_, st: (blk(bi_, st), ucol)

    def prev(blk):
        return lambda bi_, st: (jnp.maximum(blk(bi_, st) * halo_per_chunk - 1, 0),
                                ucol)

    def nxt(blk):
        return lambda bi_, st: (jnp.minimum((blk(bi_, st) + 1) * halo_per_chunk, n_halo - 1),
                                ucol)

    def out_map(blk):
        return lambda bi_, st: (blk(bi_, st), 0)

    def full(shape):
        return pl.BlockSpec(shape, lambda bi_, st: (0,) * len(shape))

    t_idx = jnp.arange(tc)
    src = (t_idx % SUBLANES) * sub + t_idx // SUBLANES
    perm = (src[:, None] == t_idx[None, :]).astype(BF16)
    permt = perm.T

    n_blk = w // LRU_BLOCK_DIM
    out = pl.pallas_call(
        functools.partial(_lru_kernel, n_chunks=n_chunks),
        out_shape=[jax.ShapeDtypeStruct((r_rows, w), BF16)] * 2,
        grid=(b, n_chunks + 1),
        in_specs=[full((tc, tc)), full((tc, tc)),
                  pl.BlockSpec((tc, w), main(fwd_blk)),
                  pl.BlockSpec((SUBLANES, w), prev(fwd_blk)),
                  pl.BlockSpec((SUBLANES, w), nxt(fwd_blk)),
                  pl.BlockSpec((tc, w), main(bwd_blk)),
                  pl.BlockSpec((SUBLANES, w), prev(bwd_blk)),
                  pl.BlockSpec((SUBLANES, w), nxt(bwd_blk)),
                  full((CONV_WIDTH, w)), full((1, w)),
                  full((2, n_blk, LRU_BLOCK_DIM, LRU_BLOCK_DIM)), full((2, w)),
                  full((2, n_blk, LRU_BLOCK_DIM, LRU_BLOCK_DIM)), full((2, w)),
                  full((2, w))],
        out_specs=[pl.BlockSpec((tc, w), out_map(fwd_blk)),
                   pl.BlockSpec((tc, w), out_map(bwd_blk))],
        scratch_shapes=[pltpu.VMEM((tc + 3 * SUBLANES, w), F32),
                        pltpu.VMEM((tc, w), F32),
                        pltpu.VMEM((tc, w), F32),
                        pltpu.VMEM((2 * SUBLANES, w), F32),
                        pltpu.VMEM((2, w), F32)],
        compiler_params=_params("arbitrary", "arbitrary"),
        name="conv_rglru",
    )(perm, permt, z_rest, z_rest, z_rest, z_rest, z_rest, z_rest,
      conv_w, conv_b.reshape(1, w), wr, br, wi, bi, lam)
    return out


def _gelu_tanh(x):
    return 0.5 * x * (1.0 + jnp.tanh(math.sqrt(2.0 / math.pi) * (x + 0.044715 * (x * x * x))))


def _rec_gate_kernel(hf_ref, hb_ref, gl_ref, o_ref):
    rec = hf_ref[...].astype(F32) + hb_ref[...].astype(F32)
    o_ref[...] = (rec * _gelu_tanh(gl_ref[...].astype(F32))).astype(o_ref.dtype)


def _rec_gate(hf, hb, z_rest, gl_col0, tm, n_rows):
    r, w = hf.shape
    assert gl_col0 % w == 0
    gl_blk = gl_col0 // w
    return pl.pallas_call(
        _rec_gate_kernel,
        out_shape=jax.ShapeDtypeStruct((r, w), BF16),
        grid=(n_rows // tm,),
        in_specs=[pl.BlockSpec((tm, w), lambda i: (i, 0)),
                  pl.BlockSpec((tm, w), lambda i: (i, 0)),
                  pl.BlockSpec((tm, w), lambda i: (i, gl_blk))],
        out_specs=pl.BlockSpec((tm, w), lambda i: (i, 0)),
        compiler_params=_params("arbitrary"),
        name="rec_gate",
    )(hf, hb, z_rest)


def _merge_kernel(att_ref, rec_ref, woa_ref, wob_ref, ga_ref, gb_ref, o_ref, woa_s, wob_s):
    @pl.when(pl.program_id(1) == 0)
    def _():
        woa_s[...] = woa_ref[...].astype(BF16)
        wob_s[...] = wob_ref[...].astype(BF16)

    ya = jnp.dot(att_ref[...], woa_s[...], preferred_element_type=F32)
    yb = jnp.dot(rec_ref[...], wob_s[...], preferred_element_type=F32)
    out = jax.nn.sigmoid(ga_ref[...].astype(F32)) * ya + jax.nn.sigmoid(gb_ref[...].astype(F32)) * yb
    o_ref[...] = out.astype(o_ref.dtype)


def _merge(att, rec, z_rest, cols, w_oa_stack, w_ob_stack, layer, n_rows, tn):
    r, qd = att.shape
    w = rec.shape[1]
    d = w_oa_stack.shape[2]
    tm = _row_tile(n_rows)
    assert cols["ga"] % tn == 0 and cols["gb"] % tn == 0
    ga_blk, gb_blk = cols["ga"] // tn, cols["gb"] // tn
    return pl.pallas_call(
        _merge_kernel,
        out_shape=jax.ShapeDtypeStruct((r, d), BF16),
        grid=(d // tn, n_rows // tm),
        in_specs=[pl.BlockSpec((tm, qd), lambda j, i: (i, 0)),
                  pl.BlockSpec((tm, w), lambda j, i: (i, 0)),
                  pl.BlockSpec((None, qd, tn), lambda j, i: (layer, 0, j)),
                  pl.BlockSpec((None, w, tn), lambda j, i: (layer, 0, j)),
                  pl.BlockSpec((tm, tn), lambda j, i: (i, ga_blk + j)),
                  pl.BlockSpec((tm, tn), lambda j, i: (i, gb_blk + j))],
        out_specs=pl.BlockSpec((tm, tn), lambda j, i: (i, j)),
        scratch_shapes=[pltpu.VMEM((qd, tn), BF16), pltpu.VMEM((w, tn), BF16)],
        compiler_params=_params("arbitrary", "arbitrary"),
        name="gated_merge",
    )(att, rec, w_oa_stack, w_ob_stack, z_rest, z_rest)


DMA_PRIORITIES = 2


def _row_copy(src_ref, src_row, dst_ref, dst_row, sem):
    return pltpu.make_async_copy(src_ref.at[pl.ds(src_row, 1)], dst_ref.at[pl.ds(dst_row, 1)], sem)


def _dispatch_kernel(pos_ref, h_ref, xs_ref, sem):
    tm = h_ref.shape[0]

    def copy(r, k):
        return _row_copy(h_ref, r, xs_ref, pos_ref[0, r * TOP_K + k], sem)

    def start(r, carry):
        for k in range(TOP_K):
            copy(r, k).start(priority=k % DMA_PRIORITIES)
        return carry

    def wait(r, carry):
        for k in range(TOP_K):
            copy(r, k).wait()
        return carry

    lax.fori_loop(0, tm, start, 0)
    lax.fori_loop(0, tm, wait, 0)


def _dispatch(hp, pos3, n_slots, tm, n_rows):
    r, dh = hp.shape
    return pl.pallas_call(
        _dispatch_kernel,
        out_shape=jax.ShapeDtypeStruct((n_slots, dh), jnp.uint32),
        grid=(n_rows // tm,),
        in_specs=[pl.BlockSpec((None, 1, tm * TOP_K), lambda i: (i, 0, 0), memory_space=pltpu.SMEM),
                  pl.BlockSpec((tm, dh), lambda i: (i, 0))],
        out_specs=pl.BlockSpec(memory_space=pl.ANY),
        scratch_shapes=[pltpu.SemaphoreType.DMA(())],
        compiler_params=_params("arbitrary"),
        name="moe_dispatch",
    )(pos3, hp)


def _expert_kernel(te_ref, ti_ref, tv_ref, xs_ref, w1_ref, b1_ref, w2_ref, ys_ref, w1_s, w2_s, *,
                   expert_dim):
    t = pl.program_id(0)
    new_expert = (t == 0) | (te_ref[t] != te_ref[jnp.maximum(t - 1, 0)])

    @pl.when(new_expert)
    def _():
        w1_s[...] = w1_ref[...].astype(BF16)
        w2_s[...] = w2_ref[...].astype(BF16)

    @pl.when(tv_ref[t] > 0)
    def _():
        p = xs_ref[...]
        row = lax.broadcasted_iota(jnp.int32, p.shape, 0)
        p = jnp.where(row < tv_ref[t], p, jnp.uint32(0))
        lo, hi = _unpack_halves(p)
        xb = jnp.concatenate([lo.astype(BF16), hi.astype(BF16)], axis=1)
        hdn = jnp.dot(xb, w1_s[...], preferred_element_type=F32) + b1_ref[...]
        glu = jnp.minimum(hdn[:, :expert_dim], SWIGLU_LIMIT)
        lin = jnp.clip(hdn[:, expert_dim:], -SWIGLU_LIMIT, SWIGLU_LIMIT)
        act = glu * jax.nn.sigmoid(SWIGLU_ALPHA * glu) * (lin + 1.0)
        y = jnp.dot(act.astype(BF16), w2_s[...], preferred_element_type=F32)
        ys_ref[...] = _pack_halves(y)


def _experts(xs, tiles, w1_stack, b1_stack, w2_stack, layer, capacity, tm):
    n_slots, dh = xs.shape
    _, n_e, d, f2 = w1_stack.shape
    f = f2 // 2
    te, ti, tv = tiles
    cap_tiles = capacity // tm

    def rows(t, te_r, ti_r, tv_r):
        return (te_r[t] * cap_tiles + ti_r[t], 0)

    grid_spec = pltpu.PrefetchScalarGridSpec(
        num_scalar_prefetch=3,
        grid=(te.shape[0],),
        in_specs=[pl.BlockSpec((tm, dh), rows),
                  pl.BlockSpec((None, None, d, f2), lambda t, te_r, ti_r, tv_r: (layer, te_r[t], 0, 0)),
                  pl.BlockSpec((None, 1, f2), lambda t, te_r, ti_r, tv_r: (layer * n_e + te_r[t], 0, 0)),
                  pl.BlockSpec((None, None, f, d), lambda t, te_r, ti_r, tv_r: (layer, te_r[t], 0, 0))],
        out_specs=pl.BlockSpec((tm, dh), rows),
        scratch_shapes=[pltpu.VMEM((d, f2), BF16), pltpu.VMEM((f, d), BF16)],
    )
    return pl.pallas_call(
        functools.partial(_expert_kernel, expert_dim=f),
        out_shape=jax.ShapeDtypeStruct((n_slots, dh), jnp.uint32),
        grid_spec=grid_spec,
        compiler_params=_params("arbitrary"),
        name="moe_experts",
    )(te, ti, tv, xs, w1_stack, b1_stack.reshape(-1, 1, f2), w2_stack)


def _combine_norm_kernel(pos_ref, pos_next_ref, wts_ref, comb_ref, b2_ref, ys_ref, x_ref, gate_ref,
                         g_ref, *refs, final):
    if final:
        o_ref, buf_s, sem = refs
    else:
        sh_ref, sc_ref, xo_ref, h_ref, buf_s, sem = refs
    i = pl.program_id(0)
    n = pl.num_programs(0)
    tm = wts_ref.shape[0]
    half = x_ref.shape[1] // 2
    slot = lax.rem(i, 2)

    def gather(p_ref, s, do_start):
        def copy(r, k):
            return _row_copy(ys_ref, p_ref[0, r * TOP_K + k], buf_s.at[s, k], r, sem.at[s])

        def body(r, carry):
            for k in range(TOP_K):
                if do_start:
                    copy(r, k).start(priority=k % DMA_PRIORITIES)
                else:
                    copy(r, k).wait()
            return carry

        lax.fori_loop(0, tm, body, 0)

    @pl.when(i == 0)
    def _():
        gather(pos_ref, slot, True)

    @pl.when(i + 1 < n)
    def _():
        gather(pos_next_ref, 1 - slot, True)

    bias = jnp.dot(comb_ref[...], b2_ref[...], precision=HIGHEST, preferred_element_type=F32)
    gather(pos_ref, slot, False)
    acc_lo = bias[:, :half]
    acc_hi = bias[:, half:]
    wts = wts_ref[...]
    for k in range(TOP_K):
        lo, hi = _unpack_halves(buf_s[slot, k])
        wk = wts[:, k:k + 1]
        acc_lo = acc_lo + wk * lo
        acc_hi = acc_hi + wk * hi
    x = x_ref[...] + gate_ref[...] * jnp.concatenate([acc_lo, acc_hi], axis=1)
    ms = jnp.mean(x * x, axis=-1, keepdims=True)
    y = (x * lax.rsqrt(ms + EPS)) * g_ref[...]
    if final:
        o_ref[...] = y
    else:
        xo_ref[...] = x
        h_ref[...] = (y * (1.0 + sc_ref[...]) + sh_ref[...]).astype(h_ref.dtype)


def _combine_norm(ys, pos3, wts, comb, b2_stack, layer, x, gate_mods, group_of_block, tm, n_rows, g,
                  mods=None):
    _, dh = ys.shape
    _, n_e, d = b2_stack.shape
    r = x.shape[0]
    final = mods is None
    n_tiles = n_rows // tm

    def row(i):
        return (i, 0)

    def mod_spec(m):
        return pl.BlockSpec((None, 1, d), lambda i: (group_of_block(i) * N_MOD + m, 0, 0))

    in_specs = [pl.BlockSpec((None, 1, tm * TOP_K), lambda i: (i, 0, 0), memory_space=pltpu.SMEM),
                pl.BlockSpec((None, 1, tm * TOP_K), lambda i: (jnp.minimum(i + 1, n_tiles - 1), 0, 0),
                             memory_space=pltpu.SMEM),
                pl.BlockSpec((tm, TOP_K), row),
                pl.BlockSpec((tm, n_e), row),
                pl.BlockSpec((None, n_e, d), lambda i: (layer, 0, 0)),
                pl.BlockSpec(memory_space=pl.ANY),
                pl.BlockSpec((tm, d), row),
                mod_spec(N_MOD - 1),
                pl.BlockSpec((1, d), lambda i: (0, 0))]
    args = [pos3, pos3, wts, comb, b2_stack, ys, x, gate_mods, g.reshape(1, d)]
    if final:
        out_shape = jax.ShapeDtypeStruct((n_rows, d), F32)
        out_specs = pl.BlockSpec((tm, d), row)
        aliases = {}
    else:
        in_specs += [mod_spec(0), mod_spec(1)]
        args += [mods, mods]
        out_shape = [jax.ShapeDtypeStruct((r, d), F32), jax.ShapeDtypeStruct((r, d), BF16)]
        out_specs = [pl.BlockSpec((tm, d), row), pl.BlockSpec((tm, d), row)]
        aliases = {6: 0}
    return pl.pallas_call(
        functools.partial(_combine_norm_kernel, final=final),
        out_shape=out_shape,
        grid=(n_tiles,),
        in_specs=in_specs,
        out_specs=out_specs,
        scratch_shapes=[pltpu.VMEM((2, TOP_K, tm, dh), jnp.uint32), pltpu.SemaphoreType.DMA((2,))],
        input_output_aliases=aliases,
        compiler_params=_params("arbitrary"),
        name="moe_combine_final" if final else "moe_combine_norm",
    )(*args)


def _expert_tiles(counts, tm, n_tiles):
    n_e = counts.shape[0]
    per = (counts + tm - 1) // tm
    ends = jnp.cumsum(per)
    total = ends[-1]
    t = jnp.minimum(jnp.arange(n_tiles, dtype=jnp.int32), total - 1)
    te = jnp.minimum(jnp.sum(t[:, None] >= ends[None, :], axis=1), n_e - 1).astype(jnp.int32)
    mine = jnp.arange(n_e, dtype=jnp.int32)[None, :] == te[:, None]
    ti = (t - jnp.sum(jnp.where(mine, (ends - per)[None, :], 0), axis=1)).astype(jnp.int32)
    tv = jnp.clip(jnp.sum(jnp.where(mine, counts[None, :], 0), axis=1) - ti * tm, 0, tm)
    tv = jnp.where(jnp.arange(n_tiles) < total, tv, 0).astype(jnp.int32)
    return te, ti, tv


def _rope_tables(b, s, c):
    rows = jnp.arange(s)
    pos_r = (rows // GRID_W).astype(F32)
    pos_c = (rows % GRID_W).astype(F32)
    axis_dim = HEAD_DIM // 2
    inv = ROPE_THETA ** (-jnp.arange(0, axis_dim, 2, dtype=F32) / axis_dim)
    ang_r = pos_r[:, None] * inv[None, :]
    ang_c = pos_c[:, None] * inv[None, :]
    ang = jnp.concatenate([ang_r, ang_r, ang_c, ang_c], axis=-1)
    sign = jnp.where((jnp.arange(HEAD_DIM) % (HEAD_DIM // 2)) < HEAD_DIM // 4, -1.0, 1.0)
    cos = jnp.concatenate([jnp.tile(jnp.cos(ang), (b, 1)), jnp.ones((b * c, HEAD_DIM), F32)], axis=0)
    sin = jnp.concatenate([jnp.tile(jnp.sin(ang) * sign, (b, 1)),
                           jnp.zeros((b * c, HEAD_DIM), F32)], axis=0)
    return cos, sin


def kernel(x, c, ctx, c_ctx, w_mod1, w_mod2, b_mod, g_mix, g_ffn, w_in, conv_w, conv_b, lru_wr,
           lru_br, lru_wi, lru_bi, lru_lam, sinks, w_oa, w_ob, w_out, w_router, b_router, w_exp1,
           b_exp1, w_exp2, b_exp2, g_final):
    b, s, d = x.shape
    cl = ctx.shape[1]
    depth = w_in.shape[0]
    lw = conv_w.shape[2]
    q_dim = N_Q_HEADS * HEAD_DIM
    kv_dim = N_KV_HEADS * HEAD_DIM
    n_lat = b * s
    n_all = n_lat + b * cl
    dims = {"B": b, "S": s, "C": cl}
    assert s % BLOCK == 0 and cl % BLOCK == 0 and n_lat % cl == 0

    tm = _tile(math.gcd(s, b * cl), 512)
    tm_norm = _tile(tm, 256)

    def group_of(tile):
        def f(i):
            return jnp.where(i * tile < n_lat, (i * tile) // s, b)
        return f

    g8 = -(-(b + 1) // SUBLANES) * SUBLANES
    cond = jnp.zeros((g8, d), F32).at[:b].set(c).at[b].set(c_ctx)
    xa = (x.reshape(n_lat, d), ctx.reshape(b * cl, d))
    cos, sin = _rope_tables(b, s, cl)

    tn = _tile(math.gcd(math.gcd(q_dim, kv_dim), math.gcd(lw, d)), 512)
    o_v = q_dim + kv_dim
    o_u = o_v + kv_dim
    n_main = (2 * lw + 2 * d) // tn
    qk_blocks = (o_v // tn, lambda j: j)
    rest_blocks = (n_main + kv_dim // tn,
                   lambda j: jnp.where(j < n_main, o_u // tn + j, o_v // tn + j - n_main))
    out_blocks = (d // tn, lambda j: j)
    rest_cols = {"u": 0, "gl": lw, "ga": 2 * lw, "gb": 2 * lw + d, "v": 2 * lw + 2 * d}

    n_experts = w_router.shape[2]
    tm_moe = tm_norm
    capacity = -(-n_all // tm_moe) * tm_moe
    n_slots = n_experts * capacity

    moe = None
    for l in range(depth):
        last = l == depth - 1
        n_rows = n_lat if last else n_all
        mods = _modulation(cond, w_mod1[l], w_mod2[l], b_mod[l]).reshape(g8 * N_MOD, 1, d)

        if moe is None:
            xa, h = _resnorm(xa, mods, group_of(tm_norm), tm_norm, n_all, g_mix[l], 0, 1)
        else:
            xa, h = _combine_norm(*moe, l - 1, xa, mods_prev, group_of(tm_moe), tm_moe, n_all,
                                  g_mix[l], mods=mods)

        z_qk = _matmul(h, w_in, l, qk_blocks, n_all, tn, rope=(cos, sin))
        z_rest = _matmul(h, w_in, l, rest_blocks, n_all, tn)

        att = _attention(z_qk, z_rest, sinks[l], None, dims, rest_cols["v"], context_queries=False)
        if not last:
            att = _attention(z_qk, z_rest, sinks[l], att, dims, rest_cols["v"], context_queries=True)
        hf, hb = _lru(z_rest, rest_cols["u"], conv_w[l], conv_b[l], lru_wr[l].astype(BF16), lru_br[l],
                      lru_wi[l].astype(BF16), lru_bi[l], lru_lam[l], dims)
        rec = _rec_gate(hf, hb, z_rest, rest_cols["gl"], tm, n_rows)
        merged = _merge(att, rec, z_rest, rest_cols, w_oa, w_ob, l, n_rows, tn)
        y = _matmul(merged, w_out, l, out_blocks, n_rows, tn)
        xa, hp, comb, pos, wts, counts = _resnorm(
            xa, mods, group_of(tm_norm), tm_norm, n_rows, g_ffn[l], 3, 4, delta=y, gate_mods=mods,
            gate_idx=2, router=(w_router[l], b_router[l]), capacity=capacity)
        pos3 = pos.reshape(n_all // tm_moe, 1, tm_moe * TOP_K)
        tiles = _expert_tiles(counts[0], tm_moe, n_rows * TOP_K // tm_moe + n_experts)
        xs = _dispatch(hp, pos3, n_slots, tm_moe, n_rows)
        ys = _experts(xs, tiles, w_exp1, b_exp1, w_exp2, l, capacity, tm_moe)
        moe = (ys, pos3, wts, comb, b_exp2)
        mods_prev = mods

    out = _combine_norm(*moe, depth - 1, xa, mods_prev, group_of(tm_moe), tm_moe, n_lat, g_final)
    return out.reshape(b, s, d)
```

```python
import functools
import math

import jax
import jax.numpy as jnp
from jax import lax
from jax.experimental import pallas as pl
from jax.experimental.pallas import tpu as pltpu

GRID_W = 64
N_Q_HEADS = 16
N_KV_HEADS = 4
HEAD_DIM = 128
WINDOW = 128
BLOCK = 128
ROPE_THETA = 10000.0
NEG_INF = -1e30
LRU_BLOCK_DIM = 128
CONV_WIDTH = 4
LRU_C = 8.0
TOP_K = 4
SWIGLU_LIMIT = 7.0
SWIGLU_ALPHA = 1.702
N_MOD = 6
EPS = 1e-6
LOG2_E = math.log2(math.e)

V7X_VMEM_LIMIT_BYTES = 56 * 1024 * 1024
SUBLANES = 8
LANES = 128

F32 = jnp.float32
BF16 = jnp.bfloat16
HIGHEST = lax.Precision.HIGHEST


def _params(*sem):
    return pltpu.CompilerParams(dimension_semantics=sem, vmem_limit_bytes=V7X_VMEM_LIMIT_BYTES)


def _tile(n, pref):
    if n <= pref:
        return n
    t = pref
    while t >= 8:
        if n % t == 0 and t % 8 == 0:
            return t
        t -= 8
    return n


def _mod_kernel(cond_ref, w1_ref, w2_ref, b_ref, o_ref, t_ref):
    @pl.when(pl.program_id(0) == 0)
    def _():
        cnd = cond_ref[...]
        t_ref[...] = jnp.dot(cnd * jax.nn.sigmoid(cnd), w1_ref[...], precision=HIGHEST,
                             preferred_element_type=F32)

    o_ref[...] = jnp.dot(t_ref[...], w2_ref[...], precision=HIGHEST,
                         preferred_element_type=F32) + b_ref[...]


def _modulation(cond, w1, w2, b):
    g8, d = cond.shape
    mr = w1.shape[1]
    n = w2.shape[1]
    tn = _tile(n, 2048)
    return pl.pallas_call(
        _mod_kernel,
        out_shape=jax.ShapeDtypeStruct((g8, n), F32),
        grid=(n // tn,),
        in_specs=[pl.BlockSpec((g8, d), lambda j: (0, 0)),
                  pl.BlockSpec((d, mr), lambda j: (0, 0)),
                  pl.BlockSpec((mr, tn), lambda j: (0, j)),
                  pl.BlockSpec((1, tn), lambda j: (0, j))],
        out_specs=pl.BlockSpec((g8, tn), lambda j: (0, j)),
        scratch_shapes=[pltpu.VMEM((g8, mr), F32)],
        compiler_params=_params("arbitrary"),
        name="modulation",
    )(cond, w1, w2, b.reshape(1, n))


def _pack_halves(v):
    half = v.shape[1] // 2

    def rounded(part):
        return lax.bitcast_convert_type(part, jnp.uint32) + jnp.uint32(0x8000)

    return (rounded(v[:, half:]) & jnp.uint32(0xFFFF0000)) | (rounded(v[:, :half]) >> jnp.uint32(16))


def _unpack_halves(p):
    lo = lax.bitcast_convert_type(p << jnp.uint32(16), F32)
    hi = lax.bitcast_convert_type(p & jnp.uint32(0xFFFF0000), F32)
    return lo, hi


def _bf16_part(v):
    bits = lax.bitcast_convert_type(v, jnp.uint32) & jnp.uint32(0xFFFF0000)
    return lax.bitcast_convert_type(bits, F32)


ROUTER_K_CHUNK = 1024


def _resnorm_kernel(*refs, has_delta, with_router, n_experts, capacity, lat_blocks):
    refs = list(refs)
    x_ref = refs.pop(0)
    if lat_blocks:
        xc_ref = refs.pop(0)
    if has_delta:
        d_ref = refs.pop(0)
        gate_ref = refs.pop(0)
    g_ref, sh_ref, sc_ref = refs.pop(0), refs.pop(0), refs.pop(0)
    if with_router:
        whi_ref, wlo_ref, br_ref = refs.pop(0), refs.pop(0), refs.pop(0)
    xo_ref, h_ref = refs.pop(0), refs.pop(0)

    x = x_ref[...]
    if lat_blocks:
        x = jnp.where(pl.program_id(0) < lat_blocks, x, xc_ref[...])
    if has_delta:
        x = x + gate_ref[...] * d_ref[...].astype(F32)
    xo_ref[...] = x
    ms = jnp.mean(x * x, axis=-1, keepdims=True)
    y = (x * lax.rsqrt(ms + EPS)) * g_ref[...]
    h = y * (1.0 + sc_ref[...]) + sh_ref[...]
    if not with_router:
        h_ref[...] = h.astype(h_ref.dtype)
        return

    h_ref[...] = _pack_halves(h)
    comb_ref, pos_ref, wts_ref, cnt_ref, run_s = refs
    tm = h.shape[0]

    @pl.when(pl.program_id(0) == 0)
    def _():
        run_s[...] = jnp.zeros_like(run_s)

    logits = br_ref[...]
    for c0 in range(0, h.shape[1], ROUTER_K_CHUNK):
        hc = h[:, c0:c0 + ROUTER_K_CHUNK]
        hi_f32 = _bf16_part(hc)
        hi = hi_f32.astype(BF16)
        lo = (hc - hi_f32).astype(BF16)
        whi = whi_ref[c0:c0 + ROUTER_K_CHUNK]
        logits = (logits + jnp.dot(hi, whi, preferred_element_type=F32)
                  + (jnp.dot(lo, whi, preferred_element_type=F32)
                     + jnp.dot(hi, wlo_ref[c0:c0 + ROUTER_K_CHUNK], preferred_element_type=F32)))
    lane = lax.broadcasted_iota(jnp.int32, logits.shape, 1)
    work = logits
    comb = jnp.zeros_like(logits)
    chosen = jnp.zeros_like(logits)
    denom = jnp.zeros((tm, 1), F32)
    top0 = None
    picks = []
    for k in range(TOP_K):
        m = jnp.max(work, axis=-1, keepdims=True)
        idx = jnp.min(jnp.where(work == m, lane, n_experts), axis=-1, keepdims=True)
        onehot = lane == idx
        if k == 0:
            top0 = m
        e = jnp.exp(m - top0)
        comb = comb + jnp.where(onehot, e, 0.0)
        chosen = chosen + jnp.where(onehot, 1.0, 0.0)
        denom = denom + e
        work = jnp.where(onehot, -jnp.inf, work)
        picks.append((idx, onehot, e))
    comb_ref[...] = comb / denom

    earlier = (lax.broadcasted_iota(jnp.int32, (tm, tm), 0)
               > lax.broadcasted_iota(jnp.int32, (tm, tm), 1))
    before = run_s[...] + jnp.dot(jnp.where(earlier, 1.0, 0.0).astype(BF16), chosen.astype(BF16),
                                  preferred_element_type=F32)
    lane_k = lax.broadcasted_iota(jnp.int32, (tm, TOP_K), 1)
    pos = jnp.zeros((tm, TOP_K), jnp.int32)
    wts = jnp.zeros((tm, TOP_K), F32)
    for k, (idx, onehot, e) in enumerate(picks):
        rank = jnp.sum(jnp.where(onehot, before, 0.0), axis=-1, keepdims=True).astype(jnp.int32)
        pos = jnp.where(lane_k == k, idx * capacity + rank, pos)
        wts = jnp.where(lane_k == k, e / denom, wts)
    pos_ref[...] = pos
    wts_ref[...] = wts
    run_s[...] = run_s[...] + jnp.sum(chosen, axis=0, keepdims=True)
    cnt_ref[...] = run_s[...].astype(jnp.int32)


def _resnorm(x, mods, group_of_block, tm, n_rows, g, shift_idx, scale_idx, *, delta=None,
             gate_mods=None, gate_idx=None, router=None, capacity=0):
    has_delta = delta is not None
    with_router = router is not None

    def row(i):
        return (i, 0)

    lat_blocks = 0
    if isinstance(x, tuple):
        x_lat, x_ctx = x
        d = x_lat.shape[1]
        r = x_lat.shape[0] + x_ctx.shape[0]
        lat_blocks = x_lat.shape[0] // tm
        in_specs = [pl.BlockSpec((tm, d), lambda i: (jnp.minimum(i, lat_blocks - 1), 0)),
                    pl.BlockSpec((tm, d), lambda i: (jnp.maximum(i - lat_blocks, 0), 0))]
        args = [x_lat, x_ctx]
    else:
        r, d = x.shape
        in_specs = [pl.BlockSpec((tm, d), row)]
        args = [x]

    def mod_spec(m):
        return pl.BlockSpec((None, 1, d), lambda i: (group_of_block(i) * N_MOD + m, 0, 0))

    if has_delta:
        in_specs += [pl.BlockSpec((tm, d), row), mod_spec(gate_idx)]
        args += [delta, gate_mods]
    in_specs += [pl.BlockSpec((1, d), lambda i: (0, 0)), mod_spec(shift_idx), mod_spec(scale_idx)]
    args += [g.reshape(1, d), mods, mods]
    out_shape = [jax.ShapeDtypeStruct((r, d), F32)]
    out_specs = [pl.BlockSpec((tm, d), row)]
    scratch = []
    n_experts = 0
    if with_router:
        w_router, b_router = router
        n_experts = w_router.shape[1]
        w_hi_f32 = _bf16_part(w_router)
        w_hi = w_hi_f32.astype(BF16)
        w_lo = (w_router - w_hi_f32).astype(BF16)
        in_specs += [pl.BlockSpec((d, n_experts), lambda i: (0, 0)),
                     pl.BlockSpec((d, n_experts), lambda i: (0, 0)),
                     pl.BlockSpec((1, n_experts), lambda i: (0, 0))]
        args += [w_hi, w_lo, b_router.reshape(1, n_experts)]
        out_shape += [jax.ShapeDtypeStruct((r, d // 2), jnp.uint32),
                      jax.ShapeDtypeStruct((r, n_experts), F32),
                      jax.ShapeDtypeStruct((r, TOP_K), jnp.int32),
                      jax.ShapeDtypeStruct((r, TOP_K), F32),
                      jax.ShapeDtypeStruct((1, n_experts), jnp.int32)]
        out_specs += [pl.BlockSpec((tm, d // 2), row),
                      pl.BlockSpec((tm, n_experts), row),
                      pl.BlockSpec((tm, TOP_K), row),
                      pl.BlockSpec((tm, TOP_K), row),
                      pl.BlockSpec((1, n_experts), lambda i: (0, 0))]
        scratch = [pltpu.VMEM((1, n_experts), F32)]
    else:
        out_shape.append(jax.ShapeDtypeStruct((r, d), BF16))
        out_specs.append(pl.BlockSpec((tm, d), row))
    return pl.pallas_call(
        functools.partial(_resnorm_kernel, has_delta=has_delta, with_router=with_router,
                          n_experts=n_experts, capacity=capacity, lat_blocks=lat_blocks),
        out_shape=out_shape,
        grid=(n_rows // tm,),
        in_specs=in_specs,
        out_specs=out_specs,
        scratch_shapes=scratch,
        input_output_aliases={} if lat_blocks else {0: 0},
        compiler_params=_params("arbitrary"),
        name="resnorm_router" if with_router else "resnorm",
    )(*args)


def _mm_kernel(a_ref, w_ref, o_ref, wb_s):
    @pl.when(pl.program_id(1) == 0)
    def _():
        wb_s[...] = w_ref[...].astype(BF16)

    o_ref[...] = jnp.dot(a_ref[...], wb_s[...], preferred_element_type=F32).astype(o_ref.dtype)


def _mm_rope_kernel(a_ref, w_ref, cos_ref, sin_ref, o_ref, wb_s):
    @pl.when(pl.program_id(1) == 0)
    def _():
        wb_s[...] = w_ref[...].astype(BF16)

    acc = jnp.dot(a_ref[...], wb_s[...], preferred_element_type=F32)
    cos = cos_ref[...]
    sin = sin_ref[...]
    lane = lax.broadcasted_iota(jnp.int32, cos.shape, 1)
    first_half = (lane % (HEAD_DIM // 2)) < (HEAD_DIM // 4)
    for c in range(acc.shape[1] // HEAD_DIM):
        xc = acc[:, c * HEAD_DIM:(c + 1) * HEAD_DIM]
        rot = jnp.where(first_half, pltpu.roll(xc, HEAD_DIM - HEAD_DIM // 4, 1),
                        pltpu.roll(xc, HEAD_DIM // 4, 1))
        o_ref[:, c * HEAD_DIM:(c + 1) * HEAD_DIM] = (xc * cos + rot * sin).astype(o_ref.dtype)


MATMUL_ROW_TILE_MAX = 1100


def _row_tile(n, max_rows=MATMUL_ROW_TILE_MAX, multiple=16):
    best = None
    for t in range(multiple, min(n, max_rows) + 1, multiple):
        if n % t == 0:
            best = t
    assert best is not None, n
    return best


def _matmul(a, w_stack, layer, col_blocks, n_rows, tn, rope=None):
    r, k = a.shape
    tm = _row_tile(n_rows)
    n_col = col_blocks[0]
    col_of = col_blocks[1]
    in_specs = [pl.BlockSpec((tm, k), lambda j, i: (i, 0)),
                pl.BlockSpec((None, k, tn), lambda j, i: (layer, 0, col_of(j)))]
    args = [a, w_stack]
    kern = _mm_kernel
    if rope is not None:
        in_specs += [pl.BlockSpec((tm, HEAD_DIM), lambda j, i: (i, 0)),
                     pl.BlockSpec((tm, HEAD_DIM), lambda j, i: (i, 0))]
        args += list(rope)
        kern = _mm_rope_kernel
    return pl.pallas_call(
        kern,
        out_shape=jax.ShapeDtypeStruct((r, n_col * tn), BF16),
        grid=(n_col, n_rows // tm),
        in_specs=in_specs,
        out_specs=pl.BlockSpec((tm, tn), lambda j, i: (i, j)),
        scratch_shapes=[pltpu.VMEM((k, tn), BF16)],
        compiler_params=_params("arbitrary", "arbitrary"),
        name="matmul_rope" if rope is not None else "matmul",
    )(*args)


def _attn_kernel(sink_ref, q_ref, *refs, n_blocks, group, has_local):
    n = pl.program_id(1)
    o_ref = refs[-1]
    if has_local:
        k_refs, v_refs = refs[0:4], refs[4:8]
        n_keys = 3 * BLOCK + k_refs[3].shape[0]
        t = lax.broadcasted_iota(jnp.int32, (BLOCK, n_keys), 0)
        col = lax.broadcasted_iota(jnp.int32, (BLOCK, n_keys), 1)
        rel = col - BLOCK - t
        lo = jnp.where(n > 0, 0, BLOCK)
        hi = jnp.where(n < n_blocks - 1, 3 * BLOCK, 2 * BLOCK)
        in_band = (jnp.abs(rel) <= WINDOW) & (col >= lo) & (col < hi)
        valid = jnp.where(col >= 3 * BLOCK, 1, in_band.astype(jnp.int32))
        valid = jnp.concatenate([valid] * group, axis=0) != 0
    else:
        k_refs, v_refs = refs[0:1], refs[1:2]
    for h in range(N_KV_HEADS):
        hs = slice(h * HEAD_DIM, (h + 1) * HEAD_DIM)
        q0 = h * group * HEAD_DIM
        qs = jnp.concatenate([q_ref[:, q0 + g * HEAD_DIM:q0 + (g + 1) * HEAD_DIM]
                              for g in range(group)], axis=0)
        kall = jnp.concatenate([r[:, hs] for r in k_refs], axis=0)
        vall = jnp.concatenate([r[:, hs] for r in v_refs], axis=0)
        s = lax.dot_general(qs, kall, (((1,), (1,)), ((), ())), preferred_element_type=F32)
        s = s * (HEAD_DIM ** -0.5 * LOG2_E)
        if has_local:
            s = jnp.where(valid, s, NEG_INF)
        sink = jnp.concatenate(
            [jnp.full((BLOCK, 1), sink_ref[h * group + g] * LOG2_E, F32) for g in range(group)],
            axis=0)
        m = jnp.maximum(jnp.max(s, axis=-1, keepdims=True), sink)
        p = jnp.exp2(s - m)
        den = jnp.sum(p, axis=-1, keepdims=True) + jnp.exp2(sink - m)
        o = jnp.dot(p.astype(vall.dtype), vall, preferred_element_type=F32) / den
        for g in range(group):
            o_ref[:, q0 + g * HEAD_DIM:q0 + (g + 1) * HEAD_DIM] = (
                o[g * BLOCK:(g + 1) * BLOCK].astype(o_ref.dtype))


def _attention(z_qk, z_rest, sinks, att_prev, dims, v_col0, context_queries):
    b, s, c = dims["B"], dims["S"], dims["C"]
    group = N_Q_HEADS // N_KV_HEADS
    q_dim = N_Q_HEADS * HEAD_DIM
    kv_dim = N_KV_HEADS * HEAD_DIM
    assert q_dim % kv_dim == 0 and v_col0 % kv_dim == 0
    kcol = q_dim // kv_dim
    vcol = v_col0 // kv_dim
    ctx_row0 = (b * s) // c

    if context_queries:
        n_blocks = c // BLOCK
        q_row0 = (b * s) // BLOCK

        def q_map(bi, n):
            return (q_row0 + bi * n_blocks + n, 0)

        in_specs = [pl.BlockSpec((BLOCK, q_dim), q_map),
                    pl.BlockSpec((c, kv_dim), lambda bi, n: (ctx_row0 + bi, kcol)),
                    pl.BlockSpec((c, kv_dim), lambda bi, n: (ctx_row0 + bi, vcol))]
        args = [z_qk, z_qk, z_rest]
    else:
        n_blocks = s // BLOCK

        def q_map(bi, n):
            return (bi * n_blocks + n, 0)

        def nb_map(off, col):
            return lambda bi, n: (bi * n_blocks + jnp.clip(n + off, 0, n_blocks - 1), col)

        blk = (BLOCK, kv_dim)
        in_specs = [pl.BlockSpec((BLOCK, q_dim), q_map)]
        in_specs += [pl.BlockSpec(blk, nb_map(off, kcol)) for off in (-1, 0, 1)]
        in_specs += [pl.BlockSpec((c, kv_dim), lambda bi, n: (ctx_row0 + bi, kcol))]
        in_specs += [pl.BlockSpec(blk, nb_map(off, vcol)) for off in (-1, 0, 1)]
        in_specs += [pl.BlockSpec((c, kv_dim), lambda bi, n: (ctx_row0 + bi, vcol))]
        args = [z_qk, z_qk, z_qk, z_qk, z_qk, z_rest, z_rest, z_rest, z_rest]

    in_specs = [pl.BlockSpec(memory_space=pltpu.SMEM)] + in_specs
    args = [sinks] + args
    aliases = {}
    if att_prev is not None:
        in_specs.append(pl.BlockSpec(memory_space=pl.ANY))
        aliases = {len(args): 0}
        args.append(att_prev)
    return pl.pallas_call(
        functools.partial(_attn_kernel, n_blocks=n_blocks, group=group,
                          has_local=not context_queries),
        out_shape=jax.ShapeDtypeStruct((z_qk.shape[0], q_dim), BF16),
        grid=(b, n_blocks),
        in_specs=in_specs,
        out_specs=pl.BlockSpec((BLOCK, q_dim), q_map),
        input_output_aliases=aliases,
        compiler_params=_params("arbitrary", "arbitrary"),
        name="attention_ctx" if context_queries else "attention",
    )(*args)


def _softplus(x):
    return jnp.maximum(x, 0.0) + jnp.log1p(jnp.exp(-jnp.abs(x)))


def _sigmoid(x):
    return 0.5 * (jnp.tanh(0.5 * x) + 1.0)


def _lru_kernel(perm_ref, permt_ref,
                uf_ref, ufp_ref, ufn_ref, ub_ref, ubp_ref, ubn_ref,
                cw_ref, cb_ref, wr_ref, br_ref, wi_ref, bi_ref, lam_ref,
                hf_ref, hb_ref,
                ext_s, a_s, b_s, end_s, st_s, *, n_chunks):
    s = pl.program_id(1)
    tc, w = uf_ref.shape
    sub = tc // SUBLANES
    n_lru_blocks = w // LRU_BLOCK_DIM

    @pl.when(s == 0)
    def _():
        st_s[...] = jnp.zeros_like(st_s)

    row8 = lax.broadcasted_iota(jnp.int32, (SUBLANES, w), 0)
    is_ctx = s == 0
    for d in range(2):
        if d == 0:
            u_ref, p_ref, n_ref, o_ref = uf_ref, ufp_ref, ufn_ref, hf_ref
            at_start = is_ctx | (s == 1)
            at_end = is_ctx | (s == n_chunks)
        else:
            u_ref, p_ref, n_ref, o_ref = ub_ref, ubp_ref, ubn_ref, hb_ref
            at_start = is_ctx | (s == n_chunks)
            at_end = is_ctx | (s == 1)

        up = jnp.dot(perm_ref[...], u_ref[...], preferred_element_type=F32)
        prev = jnp.where(at_start, 0.0, p_ref[...].astype(F32))
        nxt = jnp.where(at_end, 0.0, n_ref[...].astype(F32))
        last = up[(sub - 1) * SUBLANES:]
        last2 = up[(sub - 2) * SUBLANES:(sub - 1) * SUBLANES]
        first = up[:SUBLANES]
        ext_s[0:SUBLANES] = jnp.where(row8 == 0, prev[SUBLANES - 2:SUBLANES - 1],
                                      pltpu.roll(last2, 1, 0))
        ext_s[SUBLANES:2 * SUBLANES] = jnp.where(row8 == 0, prev[SUBLANES - 1:SUBLANES],
                                                 pltpu.roll(last, 1, 0))
        ext_s[2 * SUBLANES:2 * SUBLANES + tc] = up
        ext_s[2 * SUBLANES + tc:] = jnp.where(row8 == SUBLANES - 1, nxt[0:1],
                                              pltpu.roll(first, SUBLANES - 1, 0))

        for blk in range(n_lru_blocks):
            sl = slice(blk * LRU_BLOCK_DIM, (blk + 1) * LRU_BLOCK_DIM)
            uc = cb_ref[:, sl]
            for i in range(CONV_WIDTH):
                uc = uc + ext_s[i * SUBLANES:i * SUBLANES + tc, sl] * cw_ref[i:i + 1, sl]
            ucb = uc.astype(BF16)
            r = _sigmoid(jnp.dot(ucb, wr_ref[d, blk], preferred_element_type=F32)
                         + br_ref[d:d + 1, sl])
            gi = _sigmoid(jnp.dot(ucb, wi_ref[d, blk], preferred_element_type=F32)
                          + bi_ref[d:d + 1, sl])
            a = jnp.exp2(r * ((-LRU_C * LOG2_E) * _softplus(-lam_ref[d:d + 1, sl])))
            gap = 1.0 - a * a
            mult = jnp.where(gap > 0.0, gap * lax.rsqrt(gap), 0.0)
            a_s[:, sl] = a
            b_s[:, sl] = mult * (gi * uc)

        cols = min(4 * LANES, w)
        steps = range(sub) if d == 0 else range(sub - 1, -1, -1)
        for c0 in range(0, w, cols):
            cs = slice(c0, c0 + cols)
            hloc = jnp.zeros((SUBLANES, cols), F32)
            prod = jnp.ones((SUBLANES, cols), F32)
            for t in steps:
                rows = slice(t * SUBLANES, (t + 1) * SUBLANES)
                at = a_s[rows, cs]
                hloc = at * hloc + b_s[rows, cs]
                prod = at * prod
                b_s[rows, cs] = hloc
                a_s[rows, cs] = prod
            end_s[0:SUBLANES, cs] = hloc
            end_s[SUBLANES:, cs] = prod

        hend = end_s[0:SUBLANES]
        pend = end_s[SUBLANES:]
        carry = st_s[d:d + 1]
        carries = [None] * SUBLANES
        order = range(SUBLANES) if d == 0 else range(SUBLANES - 1, -1, -1)
        for j in order:
            carries[j] = carry
            carry = hend[j:j + 1] + pend[j:j + 1] * carry
        st_s[d:d + 1] = carry
        ctile = jnp.concatenate(carries, axis=0)
        for t in range(sub):
            rows = slice(t * SUBLANES, (t + 1) * SUBLANES)
            b_s[rows] = b_s[rows] + a_s[rows] * ctile
        o_ref[...] = jnp.dot(permt_ref[...], b_s[...].astype(BF16),
                             preferred_element_type=F32).astype(o_ref.dtype)


def _lru(z_rest, u_col0, conv_w, conv_b, wr, br, wi, bi, lam, dims):
    b, s, c = dims["B"], dims["S"], dims["C"]
    r_rows = z_rest.shape[0]
    w = conv_w.shape[1]
    tc = c
    sub = tc // SUBLANES
    assert tc % (2 * SUBLANES) == 0 and s % tc == 0 and u_col0 % w == 0
    n_chunks = s // tc
    ucol = u_col0 // w
    ctx_blk0 = (b * s) // tc
    halo_per_chunk = tc // SUBLANES
    n_halo = r_rows // SUBLANES

    def fwd_blk(bi_, st):
        return jnp.where(st == 0, ctx_blk0 + bi_, bi_ * n_chunks + st - 1)

    def bwd_blk(bi_, st):
        return jnp.where(st == 0, ctx_blk0 + bi_, bi_ * n_chunks + n_chunks - st)

    def main(blk):
        return lambda bi_, st: (blk(bi_, st), ucol)

    def prev(blk):
        return lambda bi_, st: (jnp.maximum(blk(bi_, st) * halo_per_chunk - 1, 0),
                                ucol)

    def nxt(blk):
        return lambda bi_, st: (jnp.minimum((blk(bi_, st) + 1) * halo_per_chunk, n_halo - 1),
                                ucol)

    def out_map(blk):
        return lambda bi_, st: (blk(bi_, st), 0)

    def full(shape):
        return pl.BlockSpec(shape, lambda bi_, st: (0,) * len(shape))

    t_idx = jnp.arange(tc)
    src = (t_idx % SUBLANES) * sub + t_idx // SUBLANES
    perm = (src[:, None] == t_idx[None, :]).astype(BF16)
    permt = perm.T

    n_blk = w // LRU_BLOCK_DIM
    out = pl.pallas_call(
        functools.partial(_lru_kernel, n_chunks=n_chunks),
        out_shape=[jax.ShapeDtypeStruct((r_rows, w), BF16)] * 2,
        grid=(b, n_chunks + 1),
        in_specs=[full((tc, tc)), full((tc, tc)),
                  pl.BlockSpec((tc, w), main(fwd_blk)),
                  pl.BlockSpec((SUBLANES, w), prev(fwd_blk)),
                  pl.BlockSpec((SUBLANES, w), nxt(fwd_blk)),
                  pl.BlockSpec((tc, w), main(bwd_blk)),
                  pl.BlockSpec((SUBLANES, w), prev(bwd_blk)),
                  pl.BlockSpec((SUBLANES, w), nxt(bwd_blk)),
                  full((CONV_WIDTH, w)), full((1, w)),
                  full((2, n_blk, LRU_BLOCK_DIM, LRU_BLOCK_DIM)), full((2, w)),
                  full((2, n_blk, LRU_BLOCK_DIM, LRU_BLOCK_DIM)), full((2, w)),
                  full((2, w))],
        out_specs=[pl.BlockSpec((tc, w), out_map(fwd_blk)),
                   pl.BlockSpec((tc, w), out_map(bwd_blk))],
        scratch_shapes=[pltpu.VMEM((tc + 3 * SUBLANES, w), F32),
                        pltpu.VMEM((tc, w), F32),
                        pltpu.VMEM((tc, w), F32),
                        pltpu.VMEM((2 * SUBLANES, w), F32),
                        pltpu.VMEM((2, w), F32)],
        compiler_params=_params("arbitrary", "arbitrary"),
        name="conv_rglru",
    )(perm, permt, z_rest, z_rest, z_rest, z_rest, z_rest, z_rest,
      conv_w, conv_b.reshape(1, w), wr, br, wi, bi, lam)
    return out


def _gelu_tanh(x):
    return 0.5 * x * (1.0 + jnp.tanh(math.sqrt(2.0 / math.pi) * (x + 0.044715 * (x * x * x))))


def _rec_gate_kernel(hf_ref, hb_ref, gl_ref, o_ref):
    rec = hf_ref[...].astype(F32) + hb_ref[...].astype(F32)
    o_ref[...] = (rec * _gelu_tanh(gl_ref[...].astype(F32))).astype(o_ref.dtype)


def _rec_gate(hf, hb, z_rest, gl_col0, tm, n_rows):
    r, w = hf.shape
    assert gl_col0 % w == 0
    gl_blk = gl_col0 // w
    return pl.pallas_call(
        _rec_gate_kernel,
        out_shape=jax.ShapeDtypeStruct((r, w), BF16),
        grid=(n_rows // tm,),
        in_specs=[pl.BlockSpec((tm, w), lambda i: (i, 0)),
                  pl.BlockSpec((tm, w), lambda i: (i, 0)),
                  pl.BlockSpec((tm, w), lambda i: (i, gl_blk))],
        out_specs=pl.BlockSpec((tm, w), lambda i: (i, 0)),
        compiler_params=_params("arbitrary"),
        name="rec_gate",
    )(hf, hb, z_rest)


def _merge_kernel(att_ref, rec_ref, woa_ref, wob_ref, ga_ref, gb_ref, o_ref, woa_s, wob_s):
    @pl.when(pl.program_id(1) == 0)
    def _():
        woa_s[...] = woa_ref[...].astype(BF16)
        wob_s[...] = wob_ref[...].astype(BF16)

    ya = jnp.dot(att_ref[...], woa_s[...], preferred_element_type=F32)
    yb = jnp.dot(rec_ref[...], wob_s[...], preferred_element_type=F32)
    out = _sigmoid(ga_ref[...].astype(F32)) * ya + _sigmoid(gb_ref[...].astype(F32)) * yb
    o_ref[...] = out.astype(o_ref.dtype)


def _merge(att, rec, z_rest, cols, w_oa_stack, w_ob_stack, layer, n_rows, tn):
    r, qd = att.shape
    w = rec.shape[1]
    d = w_oa_stack.shape[2]
    tm = _row_tile(n_rows)
    assert cols["ga"] % tn == 0 and cols["gb"] % tn == 0
    ga_blk, gb_blk = cols["ga"] // tn, cols["gb"] // tn
    return pl.pallas_call(
        _merge_kernel,
        out_shape=jax.ShapeDtypeStruct((r, d), BF16),
        grid=(d // tn, n_rows // tm),
        in_specs=[pl.BlockSpec((tm, qd), lambda j, i: (i, 0)),
                  pl.BlockSpec((tm, w), lambda j, i: (i, 0)),
                  pl.BlockSpec((None, qd, tn), lambda j, i: (layer, 0, j)),
                  pl.BlockSpec((None, w, tn), lambda j, i: (layer, 0, j)),
                  pl.BlockSpec((tm, tn), lambda j, i: (i, ga_blk + j)),
                  pl.BlockSpec((tm, tn), lambda j, i: (i, gb_blk + j))],
        out_specs=pl.BlockSpec((tm, tn), lambda j, i: (i, j)),
        scratch_shapes=[pltpu.VMEM((qd, tn), BF16), pltpu.VMEM((w, tn), BF16)],
        compiler_params=_params("arbitrary", "arbitrary"),
        name="gated_merge",
    )(att, rec, w_oa_stack, w_ob_stack, z_rest, z_rest)


DMA_PRIORITIES = 2


def _row_copy(src_ref, src_row, dst_ref, dst_row, sem):
    return pltpu.make_async_copy(src_ref.at[pl.ds(src_row, 1)], dst_ref.at[pl.ds(dst_row, 1)], sem)


def _dispatch_kernel(pos_ref, h_ref, xs_ref, sem):
    tm = h_ref.shape[0]

    def copy(r, k):
        return _row_copy(h_ref, r, xs_ref, pos_ref[0, r * TOP_K + k], sem)

    def start(r, carry):
        for k in range(TOP_K):
            copy(r, k).start(priority=k % DMA_PRIORITIES)
        return carry

    def wait(r, carry):
        for k in range(TOP_K):
            copy(r, k).wait()
        return carry

    lax.fori_loop(0, tm, start, 0)
    lax.fori_loop(0, tm, wait, 0)


def _dispatch(hp, pos3, n_slots, tm, n_rows):
    r, dh = hp.shape
    return pl.pallas_call(
        _dispatch_kernel,
        out_shape=jax.ShapeDtypeStruct((n_slots, dh), jnp.uint32),
        grid=(n_rows // tm,),
        in_specs=[pl.BlockSpec((None, 1, tm * TOP_K), lambda i: (i, 0, 0), memory_space=pltpu.SMEM),
                  pl.BlockSpec((tm, dh), lambda i: (i, 0))],
        out_specs=pl.BlockSpec(memory_space=pl.ANY),
        scratch_shapes=[pltpu.SemaphoreType.DMA(())],
        compiler_params=_params("arbitrary"),
        name="moe_dispatch",
    )(pos3, hp)


def _expert_kernel(te_ref, ti_ref, tv_ref, xs_ref, w1_ref, b1_ref, w2_ref, ys_ref, w1_s, w2_s, *,
                   expert_dim):
    t = pl.program_id(0)
    new_expert = (t == 0) | (te_ref[t] != te_ref[jnp.maximum(t - 1, 0)])

    @pl.when(new_expert)
    def _():
        w1_s[...] = w1_ref[...].astype(BF16)
        w2_s[...] = w2_ref[...].astype(BF16)

    @pl.when(tv_ref[t] > 0)
    def _():
        p = xs_ref[...]
        row = lax.broadcasted_iota(jnp.int32, p.shape, 0)
        p = jnp.where(row < tv_ref[t], p, jnp.uint32(0))
        lo, hi = _unpack_halves(p)
        xb = jnp.concatenate([lo.astype(BF16), hi.astype(BF16)], axis=1)
        hdn = jnp.dot(xb, w1_s[...], preferred_element_type=F32) + b1_ref[...]
        glu = jnp.minimum(hdn[:, :expert_dim], SWIGLU_LIMIT)
        lin = jnp.clip(hdn[:, expert_dim:], -SWIGLU_LIMIT, SWIGLU_LIMIT)
        act = glu * _sigmoid(SWIGLU_ALPHA * glu) * (lin + 1.0)
        y = jnp.dot(act.astype(BF16), w2_s[...], preferred_element_type=F32)
        ys_ref[...] = _pack_halves(y)


def _experts(xs, tiles, w1_stack, b1_stack, w2_stack, layer, capacity, tm):
    n_slots, dh = xs.shape
    _, n_e, d, f2 = w1_stack.shape
    f = f2 // 2
    te, ti, tv = tiles
    cap_tiles = capacity // tm

    def rows(t, te_r, ti_r, tv_r):
        return (te_r[t] * cap_tiles + ti_r[t], 0)

    grid_spec = pltpu.PrefetchScalarGridSpec(
        num_scalar_prefetch=3,
        grid=(te.shape[0],),
        in_specs=[pl.BlockSpec((tm, dh), rows),
                  pl.BlockSpec((None, None, d, f2), lambda t, te_r, ti_r, tv_r: (layer, te_r[t], 0, 0)),
                  pl.BlockSpec((None, 1, f2), lambda t, te_r, ti_r, tv_r: (layer * n_e + te_r[t], 0, 0)),
                  pl.BlockSpec((None, None, f, d), lambda t, te_r, ti_r, tv_r: (layer, te_r[t], 0, 0))],
        out_specs=pl.BlockSpec((tm, dh), rows),
        scratch_shapes=[pltpu.VMEM((d, f2), BF16), pltpu.VMEM((f, d), BF16)],
    )
    return pl.pallas_call(
        functools.partial(_expert_kernel, expert_dim=f),
        out_shape=jax.ShapeDtypeStruct((n_slots, dh), jnp.uint32),
        grid_spec=grid_spec,
        compiler_params=_params("arbitrary"),
        name="moe_experts",
    )(te, ti, tv, xs, w1_stack, b1_stack.reshape(-1, 1, f2), w2_stack)


def _combine_norm_kernel(pos_ref, pos_next_ref, wts_ref, comb_ref, b2_ref, ys_ref, x_ref, gate_ref,
                         g_ref, *refs, final):
    if final:
        o_ref, buf_s, sem = refs
    else:
        sh_ref, sc_ref, xo_ref, h_ref, buf_s, sem = refs
    i = pl.program_id(0)
    n = pl.num_programs(0)
    tm = wts_ref.shape[0]
    half = x_ref.shape[1] // 2
    slot = lax.rem(i, 2)

    def gather(p_ref, s, do_start):
        def copy(r, k):
            return _row_copy(ys_ref, p_ref[0, r * TOP_K + k], buf_s.at[s, k], r, sem.at[s])

        def body(r, carry):
            for k in range(TOP_K):
                if do_start:
                    copy(r, k).start(priority=k % DMA_PRIORITIES)
                else:
                    copy(r, k).wait()
            return carry

        lax.fori_loop(0, tm, body, 0)

    @pl.when(i == 0)
    def _():
        gather(pos_ref, slot, True)

    @pl.when(i + 1 < n)
    def _():
        gather(pos_next_ref, 1 - slot, True)

    bias = jnp.dot(comb_ref[...], b2_ref[...], precision=HIGHEST, preferred_element_type=F32)
    gather(pos_ref, slot, False)
    acc_lo = bias[:, :half]
    acc_hi = bias[:, half:]
    wts = wts_ref[...]
    for k in range(TOP_K):
        lo, hi = _unpack_halves(buf_s[slot, k])
        wk = wts[:, k:k + 1]
        acc_lo = acc_lo + wk * lo
        acc_hi = acc_hi + wk * hi
    x = x_ref[...] + gate_ref[...] * jnp.concatenate([acc_lo, acc_hi], axis=1)
    ms = jnp.mean(x * x, axis=-1, keepdims=True)
    y = (x * lax.rsqrt(ms + EPS)) * g_ref[...]
    if final:
        o_ref[...] = y
    else:
        xo_ref[...] = x
        h_ref[...] = (y * (1.0 + sc_ref[...]) + sh_ref[...]).astype(h_ref.dtype)


def _combine_norm(ys, pos3, wts, comb, b2_stack, layer, x, gate_mods, group_of_block, tm, n_rows, g,
                  mods=None):
    _, dh = ys.shape
    _, n_e, d = b2_stack.shape
    r = x.shape[0]
    final = mods is None
    n_tiles = n_rows // tm

    def row(i):
        return (i, 0)

    def mod_spec(m):
        return pl.BlockSpec((None, 1, d), lambda i: (group_of_block(i) * N_MOD + m, 0, 0))

    in_specs = [pl.BlockSpec((None, 1, tm * TOP_K), lambda i: (i, 0, 0), memory_space=pltpu.SMEM),
                pl.BlockSpec((None, 1, tm * TOP_K), lambda i: (jnp.minimum(i + 1, n_tiles - 1), 0, 0),
                             memory_space=pltpu.SMEM),
                pl.BlockSpec((tm, TOP_K), row),
                pl.BlockSpec((tm, n_e), row),
                pl.BlockSpec((None, n_e, d), lambda i: (layer, 0, 0)),
                pl.BlockSpec(memory_space=pl.ANY),
                pl.BlockSpec((tm, d), row),
                mod_spec(N_MOD - 1),
                pl.BlockSpec((1, d), lambda i: (0, 0))]
    args = [pos3, pos3, wts, comb, b2_stack, ys, x, gate_mods, g.reshape(1, d)]
    if final:
        out_shape = jax.ShapeDtypeStruct((n_rows, d), F32)
        out_specs = pl.BlockSpec((tm, d), row)
        aliases = {}
    else:
        in_specs += [mod_spec(0), mod_spec(1)]
        args += [mods, mods]
        out_shape = [jax.ShapeDtypeStruct((r, d), F32), jax.ShapeDtypeStruct((r, d), BF16)]
        out_specs = [pl.BlockSpec((tm, d), row), pl.BlockSpec((tm, d), row)]
        aliases = {6: 0}
    return pl.pallas_call(
        functools.partial(_combine_norm_kernel, final=final),
        out_shape=out_shape,
        grid=(n_tiles,),
        in_specs=in_specs,
        out_specs=out_specs,
        scratch_shapes=[pltpu.VMEM((2, TOP_K, tm, dh), jnp.uint32), pltpu.SemaphoreType.DMA((2,))],
        input_output_aliases=aliases,
        compiler_params=_params("arbitrary"),
        name="moe_combine_final" if final else "moe_combine_norm",
    )(*args)


def _expert_tiles(counts, tm, n_tiles):
    n_e = counts.shape[0]
    per = (counts + tm - 1) // tm
    ends = jnp.cumsum(per)
    total = ends[-1]
    t = jnp.minimum(jnp.arange(n_tiles, dtype=jnp.int32), total - 1)
    te = jnp.minimum(jnp.sum(t[:, None] >= ends[None, :], axis=1), n_e - 1).astype(jnp.int32)
    mine = jnp.arange(n_e, dtype=jnp.int32)[None, :] == te[:, None]
    ti = (t - jnp.sum(jnp.where(mine, (ends - per)[None, :], 0), axis=1)).astype(jnp.int32)
    tv = jnp.clip(jnp.sum(jnp.where(mine, counts[None, :], 0), axis=1) - ti * tm, 0, tm)
    tv = jnp.where(jnp.arange(n_tiles) < total, tv, 0).astype(jnp.int32)
    return te, ti, tv


def _rope_tables(b, s, c):
    rows = jnp.arange(s)
    pos_r = (rows // GRID_W).astype(F32)
    pos_c = (rows % GRID_W).astype(F32)
    axis_dim = HEAD_DIM // 2
    inv = ROPE_THETA ** (-jnp.arange(0, axis_dim, 2, dtype=F32) / axis_dim)
    ang_r = pos_r[:, None] * inv[None, :]
    ang_c = pos_c[:, None] * inv[None, :]
    ang = jnp.concatenate([ang_r, ang_r, ang_c, ang_c], axis=-1)
    sign = jnp.where((jnp.arange(HEAD_DIM) % (HEAD_DIM // 2)) < HEAD_DIM // 4, -1.0, 1.0)
    cos = jnp.concatenate([jnp.tile(jnp.cos(ang), (b, 1)), jnp.ones((b * c, HEAD_DIM), F32)], axis=0)
    sin = jnp.concatenate([jnp.tile(jnp.sin(ang) * sign, (b, 1)),
                           jnp.zeros((b * c, HEAD_DIM), F32)], axis=0)
    return cos, sin


def kernel(x, c, ctx, c_ctx, w_mod1, w_mod2, b_mod, g_mix, g_ffn, w_in, conv_w, conv_b, lru_wr,
           lru_br, lru_wi, lru_bi, lru_lam, sinks, w_oa, w_ob, w_out, w_router, b_router, w_exp1,
           b_exp1, w_exp2, b_exp2, g_final):
    b, s, d = x.shape
    cl = ctx.shape[1]
    depth = w_in.shape[0]
    lw = conv_w.shape[2]
    q_dim = N_Q_HEADS * HEAD_DIM
    kv_dim = N_KV_HEADS * HEAD_DIM
    n_lat = b * s
    n_all = n_lat + b * cl
    dims = {"B": b, "S": s, "C": cl}
    assert s % BLOCK == 0 and cl % BLOCK == 0 and n_lat % cl == 0

    tm = _tile(math.gcd(s, b * cl), 512)
    tm_norm = _tile(tm, 256)

    def group_of(tile):
        def f(i):
            return jnp.where(i * tile < n_lat, (i * tile) // s, b)
        return f

    g8 = -(-(b + 1) // SUBLANES) * SUBLANES
    cond = jnp.zeros((g8, d), F32).at[:b].set(c).at[b].set(c_ctx)
    xa = (x.reshape(n_lat, d), ctx.reshape(b * cl, d))
    cos, sin = _rope_tables(b, s, cl)

    tn = _tile(math.gcd(math.gcd(q_dim, kv_dim), math.gcd(lw, d)), 512)
    o_v = q_dim + kv_dim
    o_u = o_v + kv_dim
    n_main = (2 * lw + 2 * d) // tn
    qk_blocks = (o_v // tn, lambda j: j)
    rest_blocks = (n_main + kv_dim // tn,
                   lambda j: jnp.where(j < n_main, o_u // tn + j, o_v // tn + j - n_main))
    out_blocks = (d // tn, lambda j: j)
    rest_cols = {"u": 0, "gl": lw, "ga": 2 * lw, "gb": 2 * lw + d, "v": 2 * lw + 2 * d}

    n_experts = w_router.shape[2]
    tm_moe = tm_norm
    capacity = -(-n_all // tm_moe) * tm_moe
    n_slots = n_experts * capacity

    moe = None
    for l in range(depth):
        last = l == depth - 1
        n_rows = n_lat if last else n_all
        mods = _modulation(cond, w_mod1[l], w_mod2[l], b_mod[l]).reshape(g8 * N_MOD, 1, d)

        if moe is None:
            xa, h = _resnorm(xa, mods, group_of(tm_norm), tm_norm, n_all, g_mix[l], 0, 1)
        else:
            xa, h = _combine_norm(*moe, l - 1, xa, mods_prev, group_of(tm_moe), tm_moe, n_all,
                                  g_mix[l], mods=mods)

        z_qk = _matmul(h, w_in, l, qk_blocks, n_all, tn, rope=(cos, sin))
        z_rest = _matmul(h, w_in, l, rest_blocks, n_all, tn)

        att = _attention(z_qk, z_rest, sinks[l], None, dims, rest_cols["v"], context_queries=False)
        if not last:
            att = _attention(z_qk, z_rest, sinks[l], att, dims, rest_cols["v"], context_queries=True)
        hf, hb = _lru(z_rest, rest_cols["u"], conv_w[l], conv_b[l], lru_wr[l].astype(BF16), lru_br[l],
                      lru_wi[l].astype(BF16), lru_bi[l], lru_lam[l], dims)
        rec = _rec_gate(hf, hb, z_rest, rest_cols["gl"], tm, n_rows)
        merged = _merge(att, rec, z_rest, rest_cols, w_oa, w_ob, l, n_rows, tn)
        y = _matmul(merged, w_out, l, out_blocks, n_rows, tn)
        xa, hp, comb, pos, wts, counts = _resnorm(
            xa, mods, group_of(tm_norm), tm_norm, n_rows, g_ffn[l], 3, 4, delta=y, gate_mods=mods,
            gate_idx=2, router=(w_router[l], b_router[l]), capacity=capacity)
        pos3 = pos.reshape(n_all // tm_moe, 1, tm_moe * TOP_K)
        tiles = _expert_tiles(counts[0], tm_moe, n_rows * TOP_K // tm_moe + n_experts)
        xs = _dispatch(hp, pos3, n_slots, tm_moe, n_rows)
        ys = _experts(xs, tiles, w_exp1, b_exp1, w_exp2, l, capacity, tm_moe)
        moe = (ys, pos3, wts, comb, b_exp2)
        mods_prev = mods

    out = _combine_norm(*moe, depth - 1, xa, mods_prev, group_of(tm_moe), tm_moe, n_lat, g_final)
    return out.reshape(b, s, d)
```

```python
import functools
import math

import jax
import jax.numpy as jnp
from jax import lax
from jax.experimental import pallas as pl
from jax.experimental.pallas import tpu as pltpu

GRID_W = 64
N_Q_HEADS = 16
N_KV_HEADS = 4
HEAD_DIM = 128
WINDOW = 128
BLOCK = 128
ROPE_THETA = 10000.0
NEG_INF = -1e30
LRU_BLOCK_DIM = 128
CONV_WIDTH = 4
LRU_C = 8.0
TOP_K = 4
SWIGLU_LIMIT = 7.0
SWIGLU_ALPHA = 1.702
N_MOD = 6
EPS = 1e-6
LOG2_E = math.log2(math.e)

V7X_VMEM_LIMIT_BYTES = 56 * 1024 * 1024
SUBLANES = 8
LANES = 128

F32 = jnp.float32
BF16 = jnp.bfloat16
HIGHEST = lax.Precision.HIGHEST


def _params(*sem):
    return pltpu.CompilerParams(dimension_semantics=sem, vmem_limit_bytes=V7X_VMEM_LIMIT_BYTES)


def _tile(n, pref):
    if n <= pref:
        return n
    t = pref
    while t >= 8:
        if n % t == 0 and t % 8 == 0:
            return t
        t -= 8
    return n


def _mod_kernel(cond_ref, w1_ref, w2_ref, b_ref, o_ref, t_ref):
    @pl.when(pl.program_id(0) == 0)
    def _():
        cnd = cond_ref[...]
        t_ref[...] = jnp.dot(cnd * jax.nn.sigmoid(cnd), w1_ref[...], precision=HIGHEST,
                             preferred_element_type=F32)

    o_ref[...] = jnp.dot(t_ref[...], w2_ref[...], precision=HIGHEST,
                         preferred_element_type=F32) + b_ref[...]


def _modulation(cond, w1_stack, w2_stack, b_stack, layer):
    g8, d = cond.shape
    n_layers, _, mr = w1_stack.shape
    n = w2_stack.shape[2]
    tn = _tile(n, 2048)
    return pl.pallas_call(
        _mod_kernel,
        out_shape=jax.ShapeDtypeStruct((g8, n), F32),
        grid=(n // tn,),
        in_specs=[pl.BlockSpec((g8, d), lambda j: (0, 0)),
                  pl.BlockSpec((None, d, mr), lambda j: (layer, 0, 0)),
                  pl.BlockSpec((None, mr, tn), lambda j: (layer, 0, j)),
                  pl.BlockSpec((None, 1, tn), lambda j: (layer, 0, j))],
        out_specs=pl.BlockSpec((g8, tn), lambda j: (0, j)),
        scratch_shapes=[pltpu.VMEM((g8, mr), F32)],
        compiler_params=_params("arbitrary"),
        name="modulation",
    )(cond, w1_stack, w2_stack, b_stack.reshape(n_layers, 1, n))


def _pack_halves(v):
    half = v.shape[1] // 2

    def rounded(part):
        return lax.bitcast_convert_type(part, jnp.uint32) + jnp.uint32(0x8000)

    return (rounded(v[:, half:]) & jnp.uint32(0xFFFF0000)) | (rounded(v[:, :half]) >> jnp.uint32(16))


def _unpack_halves(p):
    lo = lax.bitcast_convert_type(p << jnp.uint32(16), F32)
    hi = lax.bitcast_convert_type(p & jnp.uint32(0xFFFF0000), F32)
    return lo, hi


def _bf16_part(v):
    bits = lax.bitcast_convert_type(v, jnp.uint32) & jnp.uint32(0xFFFF0000)
    return lax.bitcast_convert_type(bits, F32)


ROUTER_K_CHUNK = 1024


def _resnorm_kernel(*refs, has_delta, with_router, n_experts, capacity, lat_blocks):
    refs = list(refs)
    x_ref = refs.pop(0)
    if lat_blocks:
        xc_ref = refs.pop(0)
    if has_delta:
        d_ref = refs.pop(0)
        gate_ref = refs.pop(0)
    g_ref, sh_ref, sc_ref = refs.pop(0), refs.pop(0), refs.pop(0)
    if with_router:
        whi_ref, wlo_ref, br_ref = refs.pop(0), refs.pop(0), refs.pop(0)
    xo_ref, h_ref = refs.pop(0), refs.pop(0)

    x = x_ref[...]
    if lat_blocks:
        x = jnp.where(pl.program_id(0) < lat_blocks, x, xc_ref[...])
    if has_delta:
        x = x + gate_ref[...] * d_ref[...].astype(F32)
    xo_ref[...] = x
    ms = jnp.mean(x * x, axis=-1, keepdims=True)
    y = (x * lax.rsqrt(ms + EPS)) * g_ref[...]
    h = y * (1.0 + sc_ref[...]) + sh_ref[...]
    if not with_router:
        h_ref[...] = h.astype(h_ref.dtype)
        return

    h_ref[...] = _pack_halves(h)
    comb_ref, pos_ref, wts_ref, cnt_ref, run_s = refs
    tm = h.shape[0]

    @pl.when(pl.program_id(0) == 0)
    def _():
        run_s[...] = jnp.zeros_like(run_s)

    logits = br_ref[...]
    for c0 in range(0, h.shape[1], ROUTER_K_CHUNK):
        hc = h[:, c0:c0 + ROUTER_K_CHUNK]
        hi_f32 = _bf16_part(hc)
        hi = hi_f32.astype(BF16)
        lo = (hc - hi_f32).astype(BF16)
        whi = whi_ref[c0:c0 + ROUTER_K_CHUNK]
        logits = (logits + jnp.dot(hi, whi, preferred_element_type=F32)
                  + (jnp.dot(lo, whi, preferred_element_type=F32)
                     + jnp.dot(hi, wlo_ref[c0:c0 + ROUTER_K_CHUNK], preferred_element_type=F32)))
    lane = lax.broadcasted_iota(jnp.int32, logits.shape, 1)
    work = logits
    comb = jnp.zeros_like(logits)
    chosen = jnp.zeros_like(logits)
    denom = jnp.zeros((tm, 1), F32)
    top0 = None
    picks = []
    for k in range(TOP_K):
        m = jnp.max(work, axis=-1, keepdims=True)
        idx = jnp.min(jnp.where(work == m, lane, n_experts), axis=-1, keepdims=True)
        onehot = lane == idx
        if k == 0:
            top0 = m
        e = jnp.exp(m - top0)
        comb = comb + jnp.where(onehot, e, 0.0)
        chosen = chosen + jnp.where(onehot, 1.0, 0.0)
        denom = denom + e
        work = jnp.where(onehot, -jnp.inf, work)
        picks.append((idx, onehot, e))
    comb_ref[...] = comb / denom

    earlier = (lax.broadcasted_iota(jnp.int32, (tm, tm), 0)
               > lax.broadcasted_iota(jnp.int32, (tm, tm), 1))
    before = run_s[...] + jnp.dot(jnp.where(earlier, 1.0, 0.0).astype(BF16), chosen.astype(BF16),
                                  preferred_element_type=F32)
    lane_k = lax.broadcasted_iota(jnp.int32, (tm, TOP_K), 1)
    pos = jnp.zeros((tm, TOP_K), jnp.int32)
    wts = jnp.zeros((tm, TOP_K), F32)
    for k, (idx, onehot, e) in enumerate(picks):
        rank = jnp.sum(jnp.where(onehot, before, 0.0), axis=-1, keepdims=True).astype(jnp.int32)
        pos = jnp.where(lane_k == k, idx * capacity + rank, pos)
        wts = jnp.where(lane_k == k, e / denom, wts)
    pos_ref[...] = pos
    wts_ref[...] = wts
    run_s[...] = run_s[...] + jnp.sum(chosen, axis=0, keepdims=True)
    cnt_ref[...] = run_s[...].astype(jnp.int32)


def _resnorm(x, mods, group_of_block, tm, n_rows, g, shift_idx, scale_idx, *, delta=None,
             gate_mods=None, gate_idx=None, router=None, capacity=0):
    has_delta = delta is not None
    with_router = router is not None

    def row(i):
        return (i, 0)

    lat_blocks = 0
    if isinstance(x, tuple):
        x_lat, x_ctx = x
        d = x_lat.shape[1]
        r = x_lat.shape[0] + x_ctx.shape[0]
        lat_blocks = x_lat.shape[0] // tm
        in_specs = [pl.BlockSpec((tm, d), lambda i: (jnp.minimum(i, lat_blocks - 1), 0)),
                    pl.BlockSpec((tm, d), lambda i: (jnp.maximum(i - lat_blocks, 0), 0))]
        args = [x_lat, x_ctx]
    else:
        r, d = x.shape
        in_specs = [pl.BlockSpec((tm, d), row)]
        args = [x]

    def mod_spec(m):
        return pl.BlockSpec((None, 1, d), lambda i: (group_of_block(i) * N_MOD + m, 0, 0))

    if has_delta:
        in_specs += [pl.BlockSpec((tm, d), row), mod_spec(gate_idx)]
        args += [delta, gate_mods]
    in_specs += [pl.BlockSpec((1, d), lambda i: (0, 0)), mod_spec(shift_idx), mod_spec(scale_idx)]
    args += [g.reshape(1, d), mods, mods]
    out_shape = [jax.ShapeDtypeStruct((r, d), F32)]
    out_specs = [pl.BlockSpec((tm, d), row)]
    scratch = []
    n_experts = 0
    if with_router:
        w_router, b_router = router
        n_experts = w_router.shape[1]
        w_hi_f32 = _bf16_part(w_router)
        w_hi = w_hi_f32.astype(BF16)
        w_lo = (w_router - w_hi_f32).astype(BF16)
        in_specs += [pl.BlockSpec((d, n_experts), lambda i: (0, 0)),
                     pl.BlockSpec((d, n_experts), lambda i: (0, 0)),
                     pl.BlockSpec((1, n_experts), lambda i: (0, 0))]
        args += [w_hi, w_lo, b_router.reshape(1, n_experts)]
        out_shape += [jax.ShapeDtypeStruct((r, d // 2), jnp.uint32),
                      jax.ShapeDtypeStruct((r, n_experts), F32),
                      jax.ShapeDtypeStruct((r, TOP_K), jnp.int32),
                      jax.ShapeDtypeStruct((r, TOP_K), F32),
                      jax.ShapeDtypeStruct((1, n_experts), jnp.int32)]
        out_specs += [pl.BlockSpec((tm, d // 2), row),
                      pl.BlockSpec((tm, n_experts), row),
                      pl.BlockSpec((tm, TOP_K), row),
                      pl.BlockSpec((tm, TOP_K), row),
                      pl.BlockSpec((1, n_experts), lambda i: (0, 0))]
        scratch = [pltpu.VMEM((1, n_experts), F32)]
    else:
        out_shape.append(jax.ShapeDtypeStruct((r, d), BF16))
        out_specs.append(pl.BlockSpec((tm, d), row))
    return pl.pallas_call(
        functools.partial(_resnorm_kernel, has_delta=has_delta, with_router=with_router,
                          n_experts=n_experts, capacity=capacity, lat_blocks=lat_blocks),
        out_shape=out_shape,
        grid=(n_rows // tm,),
        in_specs=in_specs,
        out_specs=out_specs,
        scratch_shapes=scratch,
        input_output_aliases={} if lat_blocks else {0: 0},
        compiler_params=_params("arbitrary"),
        name="resnorm_router" if with_router else "resnorm",
    )(*args)


def _mm_kernel(a_ref, w_ref, o_ref, wb_s):
    @pl.when(pl.program_id(1) == 0)
    def _():
        wb_s[...] = w_ref[...].astype(BF16)

    o_ref[...] = jnp.dot(a_ref[...], wb_s[...], preferred_element_type=F32).astype(o_ref.dtype)


def _mm_rope_kernel(a_ref, w_ref, cos_ref, sin_ref, o_ref, wb_s):
    @pl.when(pl.program_id(1) == 0)
    def _():
        wb_s[...] = w_ref[...].astype(BF16)

    acc = jnp.dot(a_ref[...], wb_s[...], preferred_element_type=F32)
    cos = cos_ref[...]
    sin = sin_ref[...]
    lane = lax.broadcasted_iota(jnp.int32, cos.shape, 1)
    first_half = (lane % (HEAD_DIM // 2)) < (HEAD_DIM // 4)
    for c in range(acc.shape[1] // HEAD_DIM):
        xc = acc[:, c * HEAD_DIM:(c + 1) * HEAD_DIM]
        rot = jnp.where(first_half, pltpu.roll(xc, HEAD_DIM - HEAD_DIM // 4, 1),
                        pltpu.roll(xc, HEAD_DIM // 4, 1))
        o_ref[:, c * HEAD_DIM:(c + 1) * HEAD_DIM] = (xc * cos + rot * sin).astype(o_ref.dtype)


MATMUL_ROW_TILE_MAX = 1100


def _row_tile(n, max_rows=MATMUL_ROW_TILE_MAX, multiple=16):
    best = None
    for t in range(multiple, min(n, max_rows) + 1, multiple):
        if n % t == 0:
            best = t
    assert best is not None, n
    return best


def _matmul(a, w_stack, layer, col_blocks, n_rows, tn, rope=None):
    r, k = a.shape
    tm = _row_tile(n_rows)
    n_col = col_blocks[0]
    col_of = col_blocks[1]
    in_specs = [pl.BlockSpec((tm, k), lambda j, i: (i, 0)),
                pl.BlockSpec((None, k, tn), lambda j, i: (layer, 0, col_of(j)))]
    args = [a, w_stack]
    kern = _mm_kernel
    if rope is not None:
        in_specs += [pl.BlockSpec((tm, HEAD_DIM), lambda j, i: (i, 0)),
                     pl.BlockSpec((tm, HEAD_DIM), lambda j, i: (i, 0))]
        args += list(rope)
        kern = _mm_rope_kernel
    return pl.pallas_call(
        kern,
        out_shape=jax.ShapeDtypeStruct((r, n_col * tn), BF16),
        grid=(n_col, n_rows // tm),
        in_specs=in_specs,
        out_specs=pl.BlockSpec((tm, tn), lambda j, i: (i, j)),
        scratch_shapes=[pltpu.VMEM((k, tn), BF16)],
        compiler_params=_params("arbitrary", "arbitrary"),
        name="matmul_rope" if rope is not None else "matmul",
    )(*args)


def _attn_kernel(sink_ref, q_ref, *refs, n_blocks, group, has_local):
    n = pl.program_id(1)
    o_ref = refs[-1]
    if has_local:
        k_refs, v_refs = refs[0:4], refs[4:8]
        n_keys = 3 * BLOCK + k_refs[3].shape[0]
        t = lax.broadcasted_iota(jnp.int32, (BLOCK, n_keys), 0)
        col = lax.broadcasted_iota(jnp.int32, (BLOCK, n_keys), 1)
        rel = col - BLOCK - t
        lo = jnp.where(n > 0, 0, BLOCK)
        hi = jnp.where(n < n_blocks - 1, 3 * BLOCK, 2 * BLOCK)
        in_band = (jnp.abs(rel) <= WINDOW) & (col >= lo) & (col < hi)
        valid = jnp.where(col >= 3 * BLOCK, 1, in_band.astype(jnp.int32))
        valid = jnp.concatenate([valid] * group, axis=0) != 0
    else:
        k_refs, v_refs = refs[0:1], refs[1:2]
    for h in range(N_KV_HEADS):
        hs = slice(h * HEAD_DIM, (h + 1) * HEAD_DIM)
        q0 = h * group * HEAD_DIM
        qs = jnp.concatenate([q_ref[:, q0 + g * HEAD_DIM:q0 + (g + 1) * HEAD_DIM]
                              for g in range(group)], axis=0)
        kall = jnp.concatenate([r[:, hs] for r in k_refs], axis=0)
        vall = jnp.concatenate([r[:, hs] for r in v_refs], axis=0)
        s = lax.dot_general(qs, kall, (((1,), (1,)), ((), ())), preferred_element_type=F32)
        s = s * (HEAD_DIM ** -0.5 * LOG2_E)
        if has_local:
            s = jnp.where(valid, s, NEG_INF)
        sink = jnp.concatenate(
            [jnp.full((BLOCK, 1), sink_ref[h * group + g] * LOG2_E, F32) for g in range(group)],
            axis=0)
        m = jnp.maximum(jnp.max(s, axis=-1, keepdims=True), sink)
        p = jnp.exp2(s - m)
        den = jnp.sum(p, axis=-1, keepdims=True) + jnp.exp2(sink - m)
        o = jnp.dot(p.astype(vall.dtype), vall, preferred_element_type=F32) / den
        for g in range(group):
            o_ref[:, q0 + g * HEAD_DIM:q0 + (g + 1) * HEAD_DIM] = (
                o[g * BLOCK:(g + 1) * BLOCK].astype(o_ref.dtype))


def _attention(z_qk, z_rest, sinks, att_prev, dims, v_col0, context_queries):
    b, s, c = dims["B"], dims["S"], dims["C"]
    group = N_Q_HEADS // N_KV_HEADS
    q_dim = N_Q_HEADS * HEAD_DIM
    kv_dim = N_KV_HEADS * HEAD_DIM
    assert q_dim % kv_dim == 0 and v_col0 % kv_dim == 0
    kcol = q_dim // kv_dim
    vcol = v_col0 // kv_dim
    ctx_row0 = (b * s) // c

    if context_queries:
        n_blocks = c // BLOCK
        q_row0 = (b * s) // BLOCK

        def q_map(bi, n):
            return (q_row0 + bi * n_blocks + n, 0)

        in_specs = [pl.BlockSpec((BLOCK, q_dim), q_map),
                    pl.BlockSpec((c, kv_dim), lambda bi, n: (ctx_row0 + bi, kcol)),
                    pl.BlockSpec((c, kv_dim), lambda bi, n: (ctx_row0 + bi, vcol))]
        args = [z_qk, z_qk, z_rest]
    else:
        n_blocks = s // BLOCK

        def q_map(bi, n):
            return (bi * n_blocks + n, 0)

        def nb_map(off, col):
            return lambda bi, n: (bi * n_blocks + jnp.clip(n + off, 0, n_blocks - 1), col)

        blk = (BLOCK, kv_dim)
        in_specs = [pl.BlockSpec((BLOCK, q_dim), q_map)]
        in_specs += [pl.BlockSpec(blk, nb_map(off, kcol)) for off in (-1, 0, 1)]
        in_specs += [pl.BlockSpec((c, kv_dim), lambda bi, n: (ctx_row0 + bi, kcol))]
        in_specs += [pl.BlockSpec(blk, nb_map(off, vcol)) for off in (-1, 0, 1)]
        in_specs += [pl.BlockSpec((c, kv_dim), lambda bi, n: (ctx_row0 + bi, vcol))]
        args = [z_qk, z_qk, z_qk, z_qk, z_qk, z_rest, z_rest, z_rest, z_rest]

    in_specs = [pl.BlockSpec(memory_space=pltpu.SMEM)] + in_specs
    args = [sinks] + args
    aliases = {}
    if att_prev is not None:
        in_specs.append(pl.BlockSpec(memory_space=pl.ANY))
        aliases = {len(args): 0}
        args.append(att_prev)
    return pl.pallas_call(
        functools.partial(_attn_kernel, n_blocks=n_blocks, group=group,
                          has_local=not context_queries),
        out_shape=jax.ShapeDtypeStruct((z_qk.shape[0], q_dim), BF16),
        grid=(b, n_blocks),
        in_specs=in_specs,
        out_specs=pl.BlockSpec((BLOCK, q_dim), q_map),
        input_output_aliases=aliases,
        compiler_params=_params("arbitrary", "arbitrary"),
        name="attention_ctx" if context_queries else "attention",
    )(*args)


def _softplus(x):
    return jnp.maximum(x, 0.0) + jnp.log1p(jnp.exp(-jnp.abs(x)))


def _sigmoid(x):
    return 0.5 * (jnp.tanh(0.5 * x) + 1.0)


def _lru_kernel(perm_ref, permt_ref,
                uf_ref, ufp_ref, ufn_ref, ub_ref, ubp_ref, ubn_ref,
                cw_ref, cb_ref, wr_ref, br_ref, wi_ref, bi_ref, lam_ref,
                hf_ref, hb_ref,
                ext_s, a_s, b_s, end_s, st_s, *, n_chunks):
    s = pl.program_id(1)
    tc, w = uf_ref.shape
    sub = tc // SUBLANES
    n_lru_blocks = w // LRU_BLOCK_DIM

    @pl.when(s == 0)
    def _():
        st_s[...] = jnp.zeros_like(st_s)

    row8 = lax.broadcasted_iota(jnp.int32, (SUBLANES, w), 0)
    is_ctx = s == 0
    for d in range(2):
        if d == 0:
            u_ref, p_ref, n_ref, o_ref = uf_ref, ufp_ref, ufn_ref, hf_ref
            at_start = is_ctx | (s == 1)
            at_end = is_ctx | (s == n_chunks)
        else:
            u_ref, p_ref, n_ref, o_ref = ub_ref, ubp_ref, ubn_ref, hb_ref
            at_start = is_ctx | (s == n_chunks)
            at_end = is_ctx | (s == 1)

        up = jnp.dot(perm_ref[...], u_ref[...], preferred_element_type=F32)
        prev = jnp.where(at_start, 0.0, p_ref[...].astype(F32))
        nxt = jnp.where(at_end, 0.0, n_ref[...].astype(F32))
        last = up[(sub - 1) * SUBLANES:]
        last2 = up[(sub - 2) * SUBLANES:(sub - 1) * SUBLANES]
        first = up[:SUBLANES]
        ext_s[0:SUBLANES] = jnp.where(row8 == 0, prev[SUBLANES - 2:SUBLANES - 1],
                                      pltpu.roll(last2, 1, 0))
        ext_s[SUBLANES:2 * SUBLANES] = jnp.where(row8 == 0, prev[SUBLANES - 1:SUBLANES],
                                                 pltpu.roll(last, 1, 0))
        ext_s[2 * SUBLANES:2 * SUBLANES + tc] = up
        ext_s[2 * SUBLANES + tc:] = jnp.where(row8 == SUBLANES - 1, nxt[0:1],
                                              pltpu.roll(first, SUBLANES - 1, 0))

        for blk in range(n_lru_blocks):
            sl = slice(blk * LRU_BLOCK_DIM, (blk + 1) * LRU_BLOCK_DIM)
            uc = cb_ref[:, sl]
            for i in range(CONV_WIDTH):
                uc = uc + ext_s[i * SUBLANES:i * SUBLANES + tc, sl] * cw_ref[i:i + 1, sl]
            ucb = uc.astype(BF16)
            r = _sigmoid(jnp.dot(ucb, wr_ref[d, blk], preferred_element_type=F32)
                         + br_ref[d:d + 1, sl])
            gi = _sigmoid(jnp.dot(ucb, wi_ref[d, blk], preferred_element_type=F32)
                          + bi_ref[d:d + 1, sl])
            a = jnp.exp2(r * ((-LRU_C * LOG2_E) * _softplus(-lam_ref[d:d + 1, sl])))
            gap = 1.0 - a * a
            mult = jnp.where(gap > 0.0, gap * lax.rsqrt(gap), 0.0)
            a_s[:, sl] = a
            b_s[:, sl] = mult * (gi * uc)

        cols = min(4 * LANES, w)
        steps = range(sub) if d == 0 else range(sub - 1, -1, -1)
        for c0 in range(0, w, cols):
            cs = slice(c0, c0 + cols)
            hloc = jnp.zeros((SUBLANES, cols), F32)
            prod = jnp.ones((SUBLANES, cols), F32)
            for t in steps:
                rows = slice(t * SUBLANES, (t + 1) * SUBLANES)
                at = a_s[rows, cs]
                hloc = at * hloc + b_s[rows, cs]
                prod = at * prod
                b_s[rows, cs] = hloc
                a_s[rows, cs] = prod
            end_s[0:SUBLANES, cs] = hloc
            end_s[SUBLANES:, cs] = prod

        hend = end_s[0:SUBLANES]
        pend = end_s[SUBLANES:]
        carry = st_s[d:d + 1]
        carries = [None] * SUBLANES
        order = range(SUBLANES) if d == 0 else range(SUBLANES - 1, -1, -1)
        for j in order:
            carries[j] = carry
            carry = hend[j:j + 1] + pend[j:j + 1] * carry
        st_s[d:d + 1] = carry
        ctile = jnp.concatenate(carries, axis=0)
        for t in range(sub):
            rows = slice(t * SUBLANES, (t + 1) * SUBLANES)
            b_s[rows] = b_s[rows] + a_s[rows] * ctile
        o_ref[...] = jnp.dot(permt_ref[...], b_s[...].astype(BF16),
                             preferred_element_type=F32).astype(o_ref.dtype)


def _lru(z_rest, u_col0, conv_w, conv_b, wr, br, wi, bi, lam, dims):
    b, s, c = dims["B"], dims["S"], dims["C"]
    r_rows = z_rest.shape[0]
    w = conv_w.shape[1]
    tc = c
    sub = tc // SUBLANES
    assert tc % (2 * SUBLANES) == 0 and s % tc == 0 and u_col0 % w == 0
    n_chunks = s // tc
    ucol = u_col0 // w
    ctx_blk0 = (b * s) // tc
    halo_per_chunk = tc // SUBLANES
    n_halo = r_rows // SUBLANES

    def fwd_blk(bi_, st):
        return jnp.where(st == 0, ctx_blk0 + bi_, bi_ * n_chunks + st - 1)

    def bwd_blk(bi_, st):
        return jnp.where(st == 0, ctx_blk0 + bi_, bi_ * n_chunks + n_chunks - st)

    def main(blk):
        return lambda bi_, st: (blk(bi_, st), ucol)

    def prev(blk):
        return lambda bi_, st: (jnp.maximum(blk(bi_, st) * halo_per_chunk - 1, 0),
                                ucol)

    def nxt(blk):
        return lambda bi_, st: (jnp.minimum((blk(bi_, st) + 1) * halo_per_chunk, n_halo - 1),
                                ucol)

    def out_map(blk):
        return lambda bi_, st: (blk(bi_, st), 0)

    def full(shape):
        return pl.BlockSpec(shape, lambda bi_, st: (0,) * len(shape))

    t_idx = jnp.arange(tc)
    src = (t_idx % SUBLANES) * sub + t_idx // SUBLANES
    perm = (src[:, None] == t_idx[None, :]).astype(BF16)
    permt = perm.T

    n_blk = w // LRU_BLOCK_DIM
    out = pl.pallas_call(
        functools.partial(_lru_kernel, n_chunks=n_chunks),
        out_shape=[jax.ShapeDtypeStruct((r_rows, w), BF16)] * 2,
        grid=(b, n_chunks + 1),
        in_specs=[full((tc, tc)), full((tc, tc)),
                  pl.BlockSpec((tc, w), main(fwd_blk)),
                  pl.BlockSpec((SUBLANES, w), prev(fwd_blk)),
                  pl.BlockSpec((SUBLANES, w), nxt(fwd_blk)),
                  pl.BlockSpec((tc, w), main(bwd_blk)),
                  pl.BlockSpec((SUBLANES, w), prev(bwd_blk)),
                  pl.BlockSpec((SUBLANES, w), nxt(bwd_blk)),
                  full((CONV_WIDTH, w)), full((1, w)),
                  full((2, n_blk, LRU_BLOCK_DIM, LRU_BLOCK_DIM)), full((2, w)),
                  full((2, n_blk, LRU_BLOCK_DIM, LRU_BLOCK_DIM)), full((2, w)),
                  full((2, w))],
        out_specs=[pl.BlockSpec((tc, w), out_map(fwd_blk)),
                   pl.BlockSpec((tc, w), out_map(bwd_blk))],
        scratch_shapes=[pltpu.VMEM((tc + 3 * SUBLANES, w), F32),
                        pltpu.VMEM((tc, w), F32),
                        pltpu.VMEM((tc, w), F32),
                        pltpu.VMEM((2 * SUBLANES, w), F32),
                        pltpu.VMEM((2, w), F32)],
        compiler_params=_params("arbitrary", "arbitrary"),
        name="conv_rglru",
    )(perm, permt, z_rest, z_rest, z_rest, z_rest, z_rest, z_rest,
      conv_w, conv_b.reshape(1, w), wr, br, wi, bi, lam)
    return out


def _gelu_tanh(x):
    return 0.5 * x * (1.0 + jnp.tanh(math.sqrt(2.0 / math.pi) * (x + 0.044715 * (x * x * x))))


def _rec_gate_kernel(hf_ref, hb_ref, gl_ref, o_ref):
    rec = hf_ref[...].astype(F32) + hb_ref[...].astype(F32)
    o_ref[...] = (rec * _gelu_tanh(gl_ref[...].astype(F32))).astype(o_ref.dtype)


def _rec_gate(hf, hb, z_rest, gl_col0, tm, n_rows):
    r, w = hf.shape
    assert gl_col0 % w == 0
    gl_blk = gl_col0 // w
    return pl.pallas_call(
        _rec_gate_kernel,
        out_shape=jax.ShapeDtypeStruct((r, w), BF16),
        grid=(n_rows // tm,),
        in_specs=[pl.BlockSpec((tm, w), lambda i: (i, 0)),
                  pl.BlockSpec((tm, w), lambda i: (i, 0)),
                  pl.BlockSpec((tm, w), lambda i: (i, gl_blk))],
        out_specs=pl.BlockSpec((tm, w), lambda i: (i, 0)),
        compiler_params=_params("arbitrary"),
        name="rec_gate",
    )(hf, hb, z_rest)


def _merge_kernel(att_ref, rec_ref, woa_ref, wob_ref, ga_ref, gb_ref, o_ref, woa_s, wob_s):
    @pl.when(pl.program_id(1) == 0)
    def _():
        woa_s[...] = woa_ref[...].astype(BF16)
        wob_s[...] = wob_ref[...].astype(BF16)

    ya = jnp.dot(att_ref[...], woa_s[...], preferred_element_type=F32)
    yb = jnp.dot(rec_ref[...], wob_s[...], preferred_element_type=F32)
    out = _sigmoid(ga_ref[...].astype(F32)) * ya + _sigmoid(gb_ref[...].astype(F32)) * yb
    o_ref[...] = out.astype(o_ref.dtype)


def _merge(att, rec, z_rest, cols, w_oa_stack, w_ob_stack, layer, n_rows, tn):
    r, qd = att.shape
    w = rec.shape[1]
    d = w_oa_stack.shape[2]
    tm = _row_tile(n_rows)
    assert cols["ga"] % tn == 0 and cols["gb"] % tn == 0
    ga_blk, gb_blk = cols["ga"] // tn, cols["gb"] // tn
    return pl.pallas_call(
        _merge_kernel,
        out_shape=jax.ShapeDtypeStruct((r, d), BF16),
        grid=(d // tn, n_rows // tm),
        in_specs=[pl.BlockSpec((tm, qd), lambda j, i: (i, 0)),
                  pl.BlockSpec((tm, w), lambda j, i: (i, 0)),
                  pl.BlockSpec((None, qd, tn), lambda j, i: (layer, 0, j)),
                  pl.BlockSpec((None, w, tn), lambda j, i: (layer, 0, j)),
                  pl.BlockSpec((tm, tn), lambda j, i: (i, ga_blk + j)),
                  pl.BlockSpec((tm, tn), lambda j, i: (i, gb_blk + j))],
        out_specs=pl.BlockSpec((tm, tn), lambda j, i: (i, j)),
        scratch_shapes=[pltpu.VMEM((qd, tn), BF16), pltpu.VMEM((w, tn), BF16)],
        compiler_params=_params("arbitrary", "arbitrary"),
        name="gated_merge",
    )(att, rec, w_oa_stack, w_ob_stack, z_rest, z_rest)


DMA_PRIORITIES = 2


def _row_copy(src_ref, src_row, dst_ref, dst_row, sem):
    return pltpu.make_async_copy(src_ref.at[pl.ds(src_row, 1)], dst_ref.at[pl.ds(dst_row, 1)], sem)


def _dispatch_kernel(pos_ref, h_ref, xs_ref, sem):
    tm = h_ref.shape[0]

    def copy(r, k):
        return _row_copy(h_ref, r, xs_ref, pos_ref[0, r * TOP_K + k], sem)

    def start(r, carry):
        for k in range(TOP_K):
            copy(r, k).start(priority=k % DMA_PRIORITIES)
        return carry

    def wait(r, carry):
        for k in range(TOP_K):
            copy(r, k).wait()
        return carry

    lax.fori_loop(0, tm, start, 0)
    lax.fori_loop(0, tm, wait, 0)


def _dispatch(hp, pos3, n_slots, tm, n_rows):
    r, dh = hp.shape
    return pl.pallas_call(
        _dispatch_kernel,
        out_shape=jax.ShapeDtypeStruct((n_slots, dh), jnp.uint32),
        grid=(n_rows // tm,),
        in_specs=[pl.BlockSpec((None, 1, tm * TOP_K), lambda i: (i, 0, 0), memory_space=pltpu.SMEM),
                  pl.BlockSpec((tm, dh), lambda i: (i, 0))],
        out_specs=pl.BlockSpec(memory_space=pl.ANY),
        scratch_shapes=[pltpu.SemaphoreType.DMA(())],
        compiler_params=_params("arbitrary"),
        name="moe_dispatch",
    )(pos3, hp)


def _expert_kernel(te_ref, ti_ref, tv_ref, xs_ref, w1_ref, b1_ref, w2_ref, ys_ref, w1_s, w2_s, *,
                   expert_dim):
    t = pl.program_id(0)
    new_expert = (t == 0) | (te_ref[t] != te_ref[jnp.maximum(t - 1, 0)])

    @pl.when(new_expert)
    def _():
        w1_s[...] = w1_ref[...].astype(BF16)
        w2_s[...] = w2_ref[...].astype(BF16)

    @pl.when(tv_ref[t] > 0)
    def _():
        p = xs_ref[...]
        row = lax.broadcasted_iota(jnp.int32, p.shape, 0)
        p = jnp.where(row < tv_ref[t], p, jnp.uint32(0))
        lo, hi = _unpack_halves(p)
        xb = jnp.concatenate([lo.astype(BF16), hi.astype(BF16)], axis=1)
        hdn = jnp.dot(xb, w1_s[...], preferred_element_type=F32) + b1_ref[...]
        glu = jnp.minimum(hdn[:, :expert_dim], SWIGLU_LIMIT)
        lin = jnp.clip(hdn[:, expert_dim:], -SWIGLU_LIMIT, SWIGLU_LIMIT)
        act = glu * _sigmoid(SWIGLU_ALPHA * glu) * (lin + 1.0)
        y = jnp.dot(act.astype(BF16), w2_s[...], preferred_element_type=F32)
        ys_ref[...] = _pack_halves(y)


def _experts(xs, tiles, w1_stack, b1_stack, w2_stack, layer, capacity, tm):
    n_slots, dh = xs.shape
    _, n_e, d, f2 = w1_stack.shape
    f = f2 // 2
    te, ti, tv = tiles
    cap_tiles = capacity // tm

    def rows(t, te_r, ti_r, tv_r):
        return (te_r[t] * cap_tiles + ti_r[t], 0)

    grid_spec = pltpu.PrefetchScalarGridSpec(
        num_scalar_prefetch=3,
        grid=(te.shape[0],),
        in_specs=[pl.BlockSpec((tm, dh), rows),
                  pl.BlockSpec((None, None, d, f2), lambda t, te_r, ti_r, tv_r: (layer, te_r[t], 0, 0)),
                  pl.BlockSpec((None, 1, f2), lambda t, te_r, ti_r, tv_r: (layer * n_e + te_r[t], 0, 0)),
                  pl.BlockSpec((None, None, f, d), lambda t, te_r, ti_r, tv_r: (layer, te_r[t], 0, 0))],
        out_specs=pl.BlockSpec((tm, dh), rows),
        scratch_shapes=[pltpu.VMEM((d, f2), BF16), pltpu.VMEM((f, d), BF16)],
    )
    return pl.pallas_call(
        functools.partial(_expert_kernel, expert_dim=f),
        out_shape=jax.ShapeDtypeStruct((n_slots, dh), jnp.uint32),
        grid_spec=grid_spec,
        compiler_params=_params("arbitrary"),
        name="moe_experts",
    )(te, ti, tv, xs, w1_stack, b1_stack.reshape(-1, 1, f2), w2_stack)


def _combine_norm_kernel(pos_ref, pos_next_ref, wts_ref, comb_ref, b2_ref, ys_ref, x_ref, gate_ref,
                         g_ref, *refs, final):
    if final:
        o_ref, buf_s, sem = refs
    else:
        sh_ref, sc_ref, xo_ref, h_ref, buf_s, sem = refs
    i = pl.program_id(0)
    n = pl.num_programs(0)
    tm = wts_ref.shape[0]
    half = x_ref.shape[1] // 2
    slot = lax.rem(i, 2)

    def gather(p_ref, s, do_start):
        def copy(r, k):
            return _row_copy(ys_ref, p_ref[0, r * TOP_K + k], buf_s.at[s, k], r, sem.at[s])

        def body(r, carry):
            for k in range(TOP_K):
                if do_start:
                    copy(r, k).start(priority=k % DMA_PRIORITIES)
                else:
                    copy(r, k).wait()
            return carry

        lax.fori_loop(0, tm, body, 0)

    @pl.when(i == 0)
    def _():
        gather(pos_ref, slot, True)

    @pl.when(i + 1 < n)
    def _():
        gather(pos_next_ref, 1 - slot, True)

    bias = jnp.dot(comb_ref[...], b2_ref[...], precision=HIGHEST, preferred_element_type=F32)
    gather(pos_ref, slot, False)
    acc_lo = bias[:, :half]
    acc_hi = bias[:, half:]
    wts = wts_ref[...]
    for k in range(TOP_K):
        lo, hi = _unpack_halves(buf_s[slot, k])
        wk = wts[:, k:k + 1]
        acc_lo = acc_lo + wk * lo
        acc_hi = acc_hi + wk * hi
    x = x_ref[...] + gate_ref[...] * jnp.concatenate([acc_lo, acc_hi], axis=1)
    ms = jnp.mean(x * x, axis=-1, keepdims=True)
    y = (x * lax.rsqrt(ms + EPS)) * g_ref[...]
    if final:
        o_ref[...] = y
    else:
        xo_ref[...] = x
        h_ref[...] = (y * (1.0 + sc_ref[...]) + sh_ref[...]).astype(h_ref.dtype)


def _combine_norm(ys, pos3, wts, comb, b2_stack, layer, x, gate_mods, group_of_block, tm, n_rows, g,
                  mods=None):
    _, dh = ys.shape
    _, n_e, d = b2_stack.shape
    r = x.shape[0]
    final = mods is None
    n_tiles = n_rows // tm

    def row(i):
        return (i, 0)

    def mod_spec(m):
        return pl.BlockSpec((None, 1, d), lambda i: (group_of_block(i) * N_MOD + m, 0, 0))

    in_specs = [pl.BlockSpec((None, 1, tm * TOP_K), lambda i: (i, 0, 0), memory_space=pltpu.SMEM),
                pl.BlockSpec((None, 1, tm * TOP_K), lambda i: (jnp.minimum(i + 1, n_tiles - 1), 0, 0),
                             memory_space=pltpu.SMEM),
                pl.BlockSpec((tm, TOP_K), row),
                pl.BlockSpec((tm, n_e), row),
                pl.BlockSpec((None, n_e, d), lambda i: (layer, 0, 0)),
                pl.BlockSpec(memory_space=pl.ANY),
                pl.BlockSpec((tm, d), row),
                mod_spec(N_MOD - 1),
                pl.BlockSpec((1, d), lambda i: (0, 0))]
    args = [pos3, pos3, wts, comb, b2_stack, ys, x, gate_mods, g.reshape(1, d)]
    if final:
        out_shape = jax.ShapeDtypeStruct((n_rows, d), F32)
        out_specs = pl.BlockSpec((tm, d), row)
        aliases = {}
    else:
        in_specs += [mod_spec(0), mod_spec(1)]
        args += [mods, mods]
        out_shape = [jax.ShapeDtypeStruct((r, d), F32), jax.ShapeDtypeStruct((r, d), BF16)]
        out_specs = [pl.BlockSpec((tm, d), row), pl.BlockSpec((tm, d), row)]
        aliases = {6: 0}
    return pl.pallas_call(
        functools.partial(_combine_norm_kernel, final=final),
        out_shape=out_shape,
        grid=(n_tiles,),
        in_specs=in_specs,
        out_specs=out_specs,
        scratch_shapes=[pltpu.VMEM((2, TOP_K, tm, dh), jnp.uint32), pltpu.SemaphoreType.DMA((2,))],
        input_output_aliases=aliases,
        compiler_params=_params("arbitrary"),
        name="moe_combine_final" if final else "moe_combine_norm",
    )(*args)


def _expert_tiles(counts, tm, n_tiles):
    n_e = counts.shape[0]
    per = (counts + tm - 1) // tm
    ends = jnp.cumsum(per)
    total = ends[-1]
    t = jnp.minimum(jnp.arange(n_tiles, dtype=jnp.int32), total - 1)
    te = jnp.minimum(jnp.sum(t[:, None] >= ends[None, :], axis=1), n_e - 1).astype(jnp.int32)
    mine = jnp.arange(n_e, dtype=jnp.int32)[None, :] == te[:, None]
    ti = (t - jnp.sum(jnp.where(mine, (ends - per)[None, :], 0), axis=1)).astype(jnp.int32)
    tv = jnp.clip(jnp.sum(jnp.where(mine, counts[None, :], 0), axis=1) - ti * tm, 0, tm)
    tv = jnp.where(jnp.arange(n_tiles) < total, tv, 0).astype(jnp.int32)
    return te, ti, tv


def _rope_tables(b, s, c):
    rows = jnp.arange(s)
    pos_r = (rows // GRID_W).astype(F32)
    pos_c = (rows % GRID_W).astype(F32)
    axis_dim = HEAD_DIM // 2
    inv = ROPE_THETA ** (-jnp.arange(0, axis_dim, 2, dtype=F32) / axis_dim)
    ang_r = pos_r[:, None] * inv[None, :]
    ang_c = pos_c[:, None] * inv[None, :]
    ang = jnp.concatenate([ang_r, ang_r, ang_c, ang_c], axis=-1)
    sign = jnp.where((jnp.arange(HEAD_DIM) % (HEAD_DIM // 2)) < HEAD_DIM // 4, -1.0, 1.0)
    cos = jnp.concatenate([jnp.tile(jnp.cos(ang), (b, 1)), jnp.ones((b * c, HEAD_DIM), F32)], axis=0)
    sin = jnp.concatenate([jnp.tile(jnp.sin(ang) * sign, (b, 1)),
                           jnp.zeros((b * c, HEAD_DIM), F32)], axis=0)
    return cos, sin


def kernel(x, c, ctx, c_ctx, w_mod1, w_mod2, b_mod, g_mix, g_ffn, w_in, conv_w, conv_b, lru_wr,
           lru_br, lru_wi, lru_bi, lru_lam, sinks, w_oa, w_ob, w_out, w_router, b_router, w_exp1,
           b_exp1, w_exp2, b_exp2, g_final):
    b, s, d = x.shape
    cl = ctx.shape[1]
    depth = w_in.shape[0]
    lw = conv_w.shape[2]
    q_dim = N_Q_HEADS * HEAD_DIM
    kv_dim = N_KV_HEADS * HEAD_DIM
    n_lat = b * s
    n_all = n_lat + b * cl
    dims = {"B": b, "S": s, "C": cl}
    assert s % BLOCK == 0 and cl % BLOCK == 0 and n_lat % cl == 0

    tm = _tile(math.gcd(s, b * cl), 512)
    tm_norm = _tile(tm, 256)

    def group_of(tile):
        def f(i):
            return jnp.where(i * tile < n_lat, (i * tile) // s, b)
        return f

    g8 = -(-(b + 1) // SUBLANES) * SUBLANES
    cond = jnp.zeros((g8, d), F32).at[:b].set(c).at[b].set(c_ctx)
    xa = (x.reshape(n_lat, d), ctx.reshape(b * cl, d))
    cos, sin = _rope_tables(b, s, cl)

    tn = _tile(math.gcd(math.gcd(q_dim, kv_dim), math.gcd(lw, d)), 512)
    o_v = q_dim + kv_dim
    o_u = o_v + kv_dim
    n_main = (2 * lw + 2 * d) // tn
    qk_blocks = (o_v // tn, lambda j: j)
    rest_blocks = (n_main + kv_dim // tn,
                   lambda j: jnp.where(j < n_main, o_u // tn + j, o_v // tn + j - n_main))
    out_blocks = (d // tn, lambda j: j)
    rest_cols = {"u": 0, "gl": lw, "ga": 2 * lw, "gb": 2 * lw + d, "v": 2 * lw + 2 * d}

    n_experts = w_router.shape[2]
    tm_moe = tm_norm
    capacity = -(-n_all // tm_moe) * tm_moe
    n_slots = n_experts * capacity

    moe = None
    for l in range(depth):
        last = l == depth - 1
        n_rows = n_lat if last else n_all
        mods = _modulation(cond, w_mod1, w_mod2, b_mod, l).reshape(g8 * N_MOD, 1, d)

        if moe is None:
            xa, h = _resnorm(xa, mods, group_of(tm_norm), tm_norm, n_all, g_mix[l], 0, 1)
        else:
            xa, h = _combine_norm(*moe, l - 1, xa, mods_prev, group_of(tm_moe), tm_moe, n_all,
                                  g_mix[l], mods=mods)

        z_qk = _matmul(h, w_in, l, qk_blocks, n_all, tn, rope=(cos, sin))
        z_rest = _matmul(h, w_in, l, rest_blocks, n_all, tn)

        att = _attention(z_qk, z_rest, sinks[l], None, dims, rest_cols["v"], context_queries=False)
        if not last:
            att = _attention(z_qk, z_rest, sinks[l], att, dims, rest_cols["v"], context_queries=True)
        hf, hb = _lru(z_rest, rest_cols["u"], conv_w[l], conv_b[l], lru_wr[l].astype(BF16), lru_br[l],
                      lru_wi[l].astype(BF16), lru_bi[l], lru_lam[l], dims)
        rec = _rec_gate(hf, hb, z_rest, rest_cols["gl"], tm, n_rows)
        merged = _merge(att, rec, z_rest, rest_cols, w_oa, w_ob, l, n_rows, tn)
        y = _matmul(merged, w_out, l, out_blocks, n_rows, tn)
        xa, hp, comb, pos, wts, counts = _resnorm(
            xa, mods, group_of(tm_norm), tm_norm, n_rows, g_ffn[l], 3, 4, delta=y, gate_mods=mods,
            gate_idx=2, router=(w_router[l], b_router[l]), capacity=capacity)
        pos3 = pos.reshape(n_all // tm_moe, 1, tm_moe * TOP_K)
        tiles = _expert_tiles(counts[0], tm_moe, n_rows * TOP_K // tm_moe + n_experts)
        xs = _dispatch(hp, pos3, n_slots, tm_moe, n_rows)
        ys = _experts(xs, tiles, w_exp1, b_exp1, w_exp2, l, capacity, tm_moe)
        moe = (ys, pos3, wts, comb, b_exp2)
        mods_prev = mods

    out = _combine_norm(*moe, depth - 1, xa, mods_prev, group_of(tm_moe), tm_moe, n_lat, g_final)
    return out.reshape(b, s, d)
```

```python
import functools
import math

import jax
import jax.numpy as jnp
from jax import lax
from jax.experimental import pallas as pl
from jax.experimental.pallas import tpu as pltpu

GRID_W = 64
N_Q_HEADS = 16
N_KV_HEADS = 4
HEAD_DIM = 128
WINDOW = 128
BLOCK = 128
ROPE_THETA = 10000.0
NEG_INF = -1e30
LRU_BLOCK_DIM = 128
CONV_WIDTH = 4
LRU_C = 8.0
TOP_K = 4
SWIGLU_LIMIT = 7.0
SWIGLU_ALPHA = 1.702
N_MOD = 6
EPS = 1e-6
LOG2_E = math.log2(math.e)

V7X_VMEM_LIMIT_BYTES = 56 * 1024 * 1024
SUBLANES = 8
LANES = 128

ELEMENTWISE_ROW_TILE = 512
NORM_ROW_TILE = 256
WEIGHT_COL_TILE = 512
MODULATION_COL_TILE = 2048

F32 = jnp.float32
BF16 = jnp.bfloat16
HIGHEST = lax.Precision.HIGHEST


def _params(*sem):
    return pltpu.CompilerParams(dimension_semantics=sem, vmem_limit_bytes=V7X_VMEM_LIMIT_BYTES)


def _tile(n, pref):
    if n <= pref:
        return n
    t = pref
    while t >= 8:
        if n % t == 0 and t % 8 == 0:
            return t
        t -= 8
    return n


def _mod_kernel(cond_ref, w1_ref, w2_ref, b_ref, o_ref, t_ref):
    @pl.when(pl.program_id(0) == 0)
    def _():
        cnd = cond_ref[...]
        t_ref[...] = jnp.dot(cnd * jax.nn.sigmoid(cnd), w1_ref[...], precision=HIGHEST,
                             preferred_element_type=F32)

    o_ref[...] = jnp.dot(t_ref[...], w2_ref[...], precision=HIGHEST,
                         preferred_element_type=F32) + b_ref[...]


def _modulation(cond, w1_stack, w2_stack, b_stack, layer):
    g8, d = cond.shape
    n_layers, _, mr = w1_stack.shape
    n = w2_stack.shape[2]
    tn = _tile(n, MODULATION_COL_TILE)
    return pl.pallas_call(
        _mod_kernel,
        out_shape=jax.ShapeDtypeStruct((g8, n), F32),
        grid=(n // tn,),
        in_specs=[pl.BlockSpec((g8, d), lambda j: (0, 0)),
                  pl.BlockSpec((None, d, mr), lambda j: (layer, 0, 0)),
                  pl.BlockSpec((None, mr, tn), lambda j: (layer, 0, j)),
                  pl.BlockSpec((None, 1, tn), lambda j: (layer, 0, j))],
        out_specs=pl.BlockSpec((g8, tn), lambda j: (0, j)),
        scratch_shapes=[pltpu.VMEM((g8, mr), F32)],
        compiler_params=_params("arbitrary"),
        name="modulation",
    )(cond, w1_stack, w2_stack, b_stack.reshape(n_layers, 1, n))


def _pack_halves(v):
    half = v.shape[1] // 2

    def rounded(part):
        return lax.bitcast_convert_type(part, jnp.uint32) + jnp.uint32(0x8000)

    return (rounded(v[:, half:]) & jnp.uint32(0xFFFF0000)) | (rounded(v[:, :half]) >> jnp.uint32(16))


def _unpack_halves(p):
    lo = lax.bitcast_convert_type(p << jnp.uint32(16), F32)
    hi = lax.bitcast_convert_type(p & jnp.uint32(0xFFFF0000), F32)
    return lo, hi


def _bf16_part(v):
    bits = lax.bitcast_convert_type(v, jnp.uint32) & jnp.uint32(0xFFFF0000)
    return lax.bitcast_convert_type(bits, F32)


ROUTER_K_CHUNK = 1024


def _resnorm_kernel(*refs, has_delta, with_router, n_experts, capacity, lat_blocks):
    refs = list(refs)
    x_ref = refs.pop(0)
    if lat_blocks:
        xc_ref = refs.pop(0)
    if has_delta:
        d_ref = refs.pop(0)
        gate_ref = refs.pop(0)
    g_ref, sh_ref, sc_ref = refs.pop(0), refs.pop(0), refs.pop(0)
    if with_router:
        whi_ref, wlo_ref, br_ref = refs.pop(0), refs.pop(0), refs.pop(0)
    xo_ref, h_ref = refs.pop(0), refs.pop(0)

    x = x_ref[...]
    if lat_blocks:
        x = jnp.where(pl.program_id(0) < lat_blocks, x, xc_ref[...])
    if has_delta:
        x = x + gate_ref[...] * d_ref[...].astype(F32)
    xo_ref[...] = x
    ms = jnp.mean(x * x, axis=-1, keepdims=True)
    y = (x * lax.rsqrt(ms + EPS)) * g_ref[...]
    h = y * (1.0 + sc_ref[...]) + sh_ref[...]
    if not with_router:
        h_ref[...] = h.astype(h_ref.dtype)
        return

    h_ref[...] = _pack_halves(h)
    comb_ref, pos_ref, wts_ref, cnt_ref, run_s = refs
    tm = h.shape[0]

    @pl.when(pl.program_id(0) == 0)
    def _():
        run_s[...] = jnp.zeros_like(run_s)

    logits = br_ref[...]
    for c0 in range(0, h.shape[1], ROUTER_K_CHUNK):
        hc = h[:, c0:c0 + ROUTER_K_CHUNK]
        hi_f32 = _bf16_part(hc)
        hi = hi_f32.astype(BF16)
        lo = (hc - hi_f32).astype(BF16)
        whi = whi_ref[c0:c0 + ROUTER_K_CHUNK]
        logits = (logits + jnp.dot(hi, whi, preferred_element_type=F32)
                  + (jnp.dot(lo, whi, preferred_element_type=F32)
                     + jnp.dot(hi, wlo_ref[c0:c0 + ROUTER_K_CHUNK], preferred_element_type=F32)))
    lane = lax.broadcasted_iota(jnp.int32, logits.shape, 1)
    work = logits
    comb = jnp.zeros_like(logits)
    chosen = jnp.zeros_like(logits)
    denom = jnp.zeros((tm, 1), F32)
    top0 = None
    picks = []
    for k in range(TOP_K):
        m = jnp.max(work, axis=-1, keepdims=True)
        idx = jnp.min(jnp.where(work == m, lane, n_experts), axis=-1, keepdims=True)
        onehot = lane == idx
        if k == 0:
            top0 = m
        e = jnp.exp(m - top0)
        comb = comb + jnp.where(onehot, e, 0.0)
        chosen = chosen + jnp.where(onehot, 1.0, 0.0)
        denom = denom + e
        work = jnp.where(onehot, -jnp.inf, work)
        picks.append((idx, onehot, e))
    comb_ref[...] = comb / denom

    earlier = (lax.broadcasted_iota(jnp.int32, (tm, tm), 0)
               > lax.broadcasted_iota(jnp.int32, (tm, tm), 1))
    before = run_s[...] + jnp.dot(jnp.where(earlier, 1.0, 0.0).astype(BF16), chosen.astype(BF16),
                                  preferred_element_type=F32)
    lane_k = lax.broadcasted_iota(jnp.int32, (tm, TOP_K), 1)
    pos = jnp.zeros((tm, TOP_K), jnp.int32)
    wts = jnp.zeros((tm, TOP_K), F32)
    for k, (idx, onehot, e) in enumerate(picks):
        rank = jnp.sum(jnp.where(onehot, before, 0.0), axis=-1, keepdims=True).astype(jnp.int32)
        pos = jnp.where(lane_k == k, idx * capacity + rank, pos)
        wts = jnp.where(lane_k == k, e / denom, wts)
    pos_ref[...] = pos
    wts_ref[...] = wts
    run_s[...] = run_s[...] + jnp.sum(chosen, axis=0, keepdims=True)
    cnt_ref[...] = run_s[...].astype(jnp.int32)


def _resnorm(x, mods, group_of_block, tm, n_rows, g, shift_idx, scale_idx, *, delta=None,
             gate_mods=None, gate_idx=None, router=None, capacity=0):
    has_delta = delta is not None
    with_router = router is not None

    def row(i):
        return (i, 0)

    lat_blocks = 0
    if isinstance(x, tuple):
        x_lat, x_ctx = x
        d = x_lat.shape[1]
        r = x_lat.shape[0] + x_ctx.shape[0]
        lat_blocks = x_lat.shape[0] // tm
        in_specs = [pl.BlockSpec((tm, d), lambda i: (jnp.minimum(i, lat_blocks - 1), 0)),
                    pl.BlockSpec((tm, d), lambda i: (jnp.maximum(i - lat_blocks, 0), 0))]
        args = [x_lat, x_ctx]
    else:
        r, d = x.shape
        in_specs = [pl.BlockSpec((tm, d), row)]
        args = [x]

    def mod_spec(m):
        return pl.BlockSpec((None, 1, d), lambda i: (group_of_block(i) * N_MOD + m, 0, 0))

    if has_delta:
        in_specs += [pl.BlockSpec((tm, d), row), mod_spec(gate_idx)]
        args += [delta, gate_mods]
    in_specs += [pl.BlockSpec((1, d), lambda i: (0, 0)), mod_spec(shift_idx), mod_spec(scale_idx)]
    args += [g.reshape(1, d), mods, mods]
    out_shape = [jax.ShapeDtypeStruct((r, d), F32)]
    out_specs = [pl.BlockSpec((tm, d), row)]
    scratch = []
    n_experts = 0
    if with_router:
        w_router, b_router = router
        n_experts = w_router.shape[1]
        w_hi_f32 = _bf16_part(w_router)
        w_hi = w_hi_f32.astype(BF16)
        w_lo = (w_router - w_hi_f32).astype(BF16)
        in_specs += [pl.BlockSpec((d, n_experts), lambda i: (0, 0)),
                     pl.BlockSpec((d, n_experts), lambda i: (0, 0)),
                     pl.BlockSpec((1, n_experts), lambda i: (0, 0))]
        args += [w_hi, w_lo, b_router.reshape(1, n_experts)]
        out_shape += [jax.ShapeDtypeStruct((r, d // 2), jnp.uint32),
                      jax.ShapeDtypeStruct((r, n_experts), F32),
                      jax.ShapeDtypeStruct((r, TOP_K), jnp.int32),
                      jax.ShapeDtypeStruct((r, TOP_K), F32),
                      jax.ShapeDtypeStruct((1, n_experts), jnp.int32)]
        out_specs += [pl.BlockSpec((tm, d // 2), row),
                      pl.BlockSpec((tm, n_experts), row),
                      pl.BlockSpec((tm, TOP_K), row),
                      pl.BlockSpec((tm, TOP_K), row),
                      pl.BlockSpec((1, n_experts), lambda i: (0, 0))]
        scratch = [pltpu.VMEM((1, n_experts), F32)]
    else:
        out_shape.append(jax.ShapeDtypeStruct((r, d), BF16))
        out_specs.append(pl.BlockSpec((tm, d), row))
    return pl.pallas_call(
        functools.partial(_resnorm_kernel, has_delta=has_delta, with_router=with_router,
                          n_experts=n_experts, capacity=capacity, lat_blocks=lat_blocks),
        out_shape=out_shape,
        grid=(n_rows // tm,),
        in_specs=in_specs,
        out_specs=out_specs,
        scratch_shapes=scratch,
        input_output_aliases={} if lat_blocks else {0: 0},
        compiler_params=_params("arbitrary"),
        name="resnorm_router" if with_router else "resnorm",
    )(*args)


def _mm_kernel(a_ref, w_ref, o_ref, wb_s):
    @pl.when(pl.program_id(1) == 0)
    def _():
        wb_s[...] = w_ref[...].astype(BF16)

    o_ref[...] = jnp.dot(a_ref[...], wb_s[...], preferred_element_type=F32).astype(o_ref.dtype)


def _mm_rope_kernel(a_ref, w_ref, cos_ref, sin_ref, o_ref, wb_s):
    @pl.when(pl.program_id(1) == 0)
    def _():
        wb_s[...] = w_ref[...].astype(BF16)

    acc = jnp.dot(a_ref[...], wb_s[...], preferred_element_type=F32)
    cos = cos_ref[...]
    sin = sin_ref[...]
    lane = lax.broadcasted_iota(jnp.int32, cos.shape, 1)
    first_half = (lane % (HEAD_DIM // 2)) < (HEAD_DIM // 4)
    for c in range(acc.shape[1] // HEAD_DIM):
        xc = acc[:, c * HEAD_DIM:(c + 1) * HEAD_DIM]
        rot = jnp.where(first_half, pltpu.roll(xc, HEAD_DIM - HEAD_DIM // 4, 1),
                        pltpu.roll(xc, HEAD_DIM // 4, 1))
        o_ref[:, c * HEAD_DIM:(c + 1) * HEAD_DIM] = (xc * cos + rot * sin).astype(o_ref.dtype)


MATMUL_ROW_TILE_MAX = 1100


def _row_tile(n, max_rows=MATMUL_ROW_TILE_MAX, multiple=16):
    best = None
    for t in range(multiple, min(n, max_rows) + 1, multiple):
        if n % t == 0:
            best = t
    assert best is not None, n
    return best


def _matmul(a, w_stack, layer, col_blocks, n_rows, tn, rope=None):
    r, k = a.shape
    tm = _row_tile(n_rows)
    n_col = col_blocks[0]
    col_of = col_blocks[1]
    in_specs = [pl.BlockSpec((tm, k), lambda j, i: (i, 0)),
                pl.BlockSpec((None, k, tn), lambda j, i: (layer, 0, col_of(j)))]
    args = [a, w_stack]
    kern = _mm_kernel
    if rope is not None:
        in_specs += [pl.BlockSpec((tm, HEAD_DIM), lambda j, i: (i, 0)),
                     pl.BlockSpec((tm, HEAD_DIM), lambda j, i: (i, 0))]
        args += list(rope)
        kern = _mm_rope_kernel
    return pl.pallas_call(
        kern,
        out_shape=jax.ShapeDtypeStruct((r, n_col * tn), BF16),
        grid=(n_col, n_rows // tm),
        in_specs=in_specs,
        out_specs=pl.BlockSpec((tm, tn), lambda j, i: (i, j)),
        scratch_shapes=[pltpu.VMEM((k, tn), BF16)],
        compiler_params=_params("arbitrary", "arbitrary"),
        name="matmul_rope" if rope is not None else "matmul",
    )(*args)


def _attn_kernel(sink_ref, q_ref, *refs, n_blocks, group, has_local):
    n = pl.program_id(1)
    o_ref = refs[-1]
    if has_local:
        k_refs, v_refs = refs[0:4], refs[4:8]
        n_keys = 3 * BLOCK + k_refs[3].shape[0]
        t = lax.broadcasted_iota(jnp.int32, (BLOCK, n_keys), 0)
        col = lax.broadcasted_iota(jnp.int32, (BLOCK, n_keys), 1)
        rel = col - BLOCK - t
        lo = jnp.where(n > 0, 0, BLOCK)
        hi = jnp.where(n < n_blocks - 1, 3 * BLOCK, 2 * BLOCK)
        in_band = (jnp.abs(rel) <= WINDOW) & (col >= lo) & (col < hi)
        valid = jnp.where(col >= 3 * BLOCK, 1, in_band.astype(jnp.int32))
        valid = jnp.concatenate([valid] * group, axis=0) != 0
    else:
        k_refs, v_refs = refs[0:1], refs[1:2]
    for h in range(N_KV_HEADS):
        hs = slice(h * HEAD_DIM, (h + 1) * HEAD_DIM)
        q0 = h * group * HEAD_DIM
        qs = jnp.concatenate([q_ref[:, q0 + g * HEAD_DIM:q0 + (g + 1) * HEAD_DIM]
                              for g in range(group)], axis=0)
        kall = jnp.concatenate([r[:, hs] for r in k_refs], axis=0)
        vall = jnp.concatenate([r[:, hs] for r in v_refs], axis=0)
        s = lax.dot_general(qs, kall, (((1,), (1,)), ((), ())), preferred_element_type=F32)
        s = s * (HEAD_DIM ** -0.5 * LOG2_E)
        if has_local:
            s = jnp.where(valid, s, NEG_INF)
        sink = jnp.concatenate(
            [jnp.full((BLOCK, 1), sink_ref[h * group + g] * LOG2_E, F32) for g in range(group)],
            axis=0)
        m = jnp.maximum(jnp.max(s, axis=-1, keepdims=True), sink)
        p = jnp.exp2(s - m)
        den = jnp.sum(p, axis=-1, keepdims=True) + jnp.exp2(sink - m)
        o = jnp.dot(p.astype(vall.dtype), vall, preferred_element_type=F32) / den
        for g in range(group):
            o_ref[:, q0 + g * HEAD_DIM:q0 + (g + 1) * HEAD_DIM] = (
                o[g * BLOCK:(g + 1) * BLOCK].astype(o_ref.dtype))


def _attention(z_qk, z_rest, sinks, att_prev, dims, v_col0, context_queries):
    b, s, c = dims["B"], dims["S"], dims["C"]
    group = N_Q_HEADS // N_KV_HEADS
    q_dim = N_Q_HEADS * HEAD_DIM
    kv_dim = N_KV_HEADS * HEAD_DIM
    assert q_dim % kv_dim == 0 and v_col0 % kv_dim == 0
    kcol = q_dim // kv_dim
    vcol = v_col0 // kv_dim
    ctx_row0 = (b * s) // c

    if context_queries:
        n_blocks = c // BLOCK
        q_row0 = (b * s) // BLOCK

        def q_map(bi, n):
            return (q_row0 + bi * n_blocks + n, 0)

        in_specs = [pl.BlockSpec((BLOCK, q_dim), q_map),
                    pl.BlockSpec((c, kv_dim), lambda bi, n: (ctx_row0 + bi, kcol)),
                    pl.BlockSpec((c, kv_dim), lambda bi, n: (ctx_row0 + bi, vcol))]
        args = [z_qk, z_qk, z_rest]
    else:
        n_blocks = s // BLOCK

        def q_map(bi, n):
            return (bi * n_blocks + n, 0)

        def nb_map(off, col):
            return lambda bi, n: (bi * n_blocks + jnp.clip(n + off, 0, n_blocks - 1), col)

        blk = (BLOCK, kv_dim)
        in_specs = [pl.BlockSpec((BLOCK, q_dim), q_map)]
        in_specs += [pl.BlockSpec(blk, nb_map(off, kcol)) for off in (-1, 0, 1)]
        in_specs += [pl.BlockSpec((c, kv_dim), lambda bi, n: (ctx_row0 + bi, kcol))]
        in_specs += [pl.BlockSpec(blk, nb_map(off, vcol)) for off in (-1, 0, 1)]
        in_specs += [pl.BlockSpec((c, kv_dim), lambda bi, n: (ctx_row0 + bi, vcol))]
        args = [z_qk, z_qk, z_qk, z_qk, z_qk, z_rest, z_rest, z_rest, z_rest]

    in_specs = [pl.BlockSpec(memory_space=pltpu.SMEM)] + in_specs
    args = [sinks] + args
    aliases = {}
    if att_prev is not None:
        in_specs.append(pl.BlockSpec(memory_space=pl.ANY))
        aliases = {len(args): 0}
        args.append(att_prev)
    return pl.pallas_call(
        functools.partial(_attn_kernel, n_blocks=n_blocks, group=group,
                          has_local=not context_queries),
        out_shape=jax.ShapeDtypeStruct((z_qk.shape[0], q_dim), BF16),
        grid=(b, n_blocks),
        in_specs=in_specs,
        out_specs=pl.BlockSpec((BLOCK, q_dim), q_map),
        input_output_aliases=aliases,
        compiler_params=_params("arbitrary", "arbitrary"),
        name="attention_ctx" if context_queries else "attention",
    )(*args)


def _softplus(x):
    return jnp.maximum(x, 0.0) + jnp.log1p(jnp.exp(-jnp.abs(x)))


def _sigmoid(x):
    return 0.5 * (jnp.tanh(0.5 * x) + 1.0)


def _lru_kernel(perm_ref, permt_ref,
                uf_ref, ufp_ref, ufn_ref, ub_ref, ubp_ref, ubn_ref,
                cw_ref, cb_ref, wr_ref, br_ref, wi_ref, bi_ref, lam_ref,
                hf_ref, hb_ref,
                ext_s, a_s, b_s, end_s, st_s, *, n_chunks):
    s = pl.program_id(1)
    tc, w = uf_ref.shape
    sub = tc // SUBLANES
    n_lru_blocks = w // LRU_BLOCK_DIM

    @pl.when(s == 0)
    def _():
        st_s[...] = jnp.zeros_like(st_s)

    row8 = lax.broadcasted_iota(jnp.int32, (SUBLANES, w), 0)
    is_ctx = s == 0
    for d in range(2):
        if d == 0:
            u_ref, p_ref, n_ref, o_ref = uf_ref, ufp_ref, ufn_ref, hf_ref
            at_start = is_ctx | (s == 1)
            at_end = is_ctx | (s == n_chunks)
        else:
            u_ref, p_ref, n_ref, o_ref = ub_ref, ubp_ref, ubn_ref, hb_ref
            at_start = is_ctx | (s == n_chunks)
            at_end = is_ctx | (s == 1)

        up = jnp.dot(perm_ref[...], u_ref[...], preferred_element_type=F32)
        prev = jnp.where(at_start, 0.0, p_ref[...].astype(F32))
        nxt = jnp.where(at_end, 0.0, n_ref[...].astype(F32))
        last = up[(sub - 1) * SUBLANES:]
        last2 = up[(sub - 2) * SUBLANES:(sub - 1) * SUBLANES]
        first = up[:SUBLANES]
        ext_s[0:SUBLANES] = jnp.where(row8 == 0, prev[SUBLANES - 2:SUBLANES - 1],
                                      pltpu.roll(last2, 1, 0))
        ext_s[SUBLANES:2 * SUBLANES] = jnp.where(row8 == 0, prev[SUBLANES - 1:SUBLANES],
                                                 pltpu.roll(last, 1, 0))
        ext_s[2 * SUBLANES:2 * SUBLANES + tc] = up
        ext_s[2 * SUBLANES + tc:] = jnp.where(row8 == SUBLANES - 1, nxt[0:1],
                                              pltpu.roll(first, SUBLANES - 1, 0))

        for blk in range(n_lru_blocks):
            sl = slice(blk * LRU_BLOCK_DIM, (blk + 1) * LRU_BLOCK_DIM)
            uc = cb_ref[:, sl]
            for i in range(CONV_WIDTH):
                uc = uc + ext_s[i * SUBLANES:i * SUBLANES + tc, sl] * cw_ref[i:i + 1, sl]
            ucb = uc.astype(BF16)
            r = _sigmoid(jnp.dot(ucb, wr_ref[d, blk], preferred_element_type=F32)
                         + br_ref[d:d + 1, sl])
            gi = _sigmoid(jnp.dot(ucb, wi_ref[d, blk], preferred_element_type=F32)
                          + bi_ref[d:d + 1, sl])
            a = jnp.exp2(r * ((-LRU_C * LOG2_E) * _softplus(-lam_ref[d:d + 1, sl])))
            gap = 1.0 - a * a
            mult = jnp.where(gap > 0.0, gap * lax.rsqrt(gap), 0.0)
            a_s[:, sl] = a
            b_s[:, sl] = mult * (gi * uc)

        cols = min(4 * LANES, w)
        steps = range(sub) if d == 0 else range(sub - 1, -1, -1)
        for c0 in range(0, w, cols):
            cs = slice(c0, c0 + cols)
            hloc = jnp.zeros((SUBLANES, cols), F32)
            prod = jnp.ones((SUBLANES, cols), F32)
            for t in steps:
                rows = slice(t * SUBLANES, (t + 1) * SUBLANES)
                at = a_s[rows, cs]
                hloc = at * hloc + b_s[rows, cs]
                prod = at * prod
                b_s[rows, cs] = hloc
                a_s[rows, cs] = prod
            end_s[0:SUBLANES, cs] = hloc
            end_s[SUBLANES:, cs] = prod

        hend = end_s[0:SUBLANES]
        pend = end_s[SUBLANES:]
        carry = st_s[d:d + 1]
        carries = [None] * SUBLANES
        order = range(SUBLANES) if d == 0 else range(SUBLANES - 1, -1, -1)
        for j in order:
            carries[j] = carry
            carry = hend[j:j + 1] + pend[j:j + 1] * carry
        st_s[d:d + 1] = carry
        ctile = jnp.concatenate(carries, axis=0)
        for t in range(sub):
            rows = slice(t * SUBLANES, (t + 1) * SUBLANES)
            b_s[rows] = b_s[rows] + a_s[rows] * ctile
        o_ref[...] = jnp.dot(permt_ref[...], b_s[...].astype(BF16),
                             preferred_element_type=F32).astype(o_ref.dtype)


def _lru(z_rest, u_col0, conv_w, conv_b, wr, br, wi, bi, lam, dims):
    b, s, c = dims["B"], dims["S"], dims["C"]
    r_rows = z_rest.shape[0]
    w = conv_w.shape[1]
    tc = c
    sub = tc // SUBLANES
    assert tc % (2 * SUBLANES) == 0 and s % tc == 0 and u_col0 % w == 0
    n_chunks = s // tc
    ucol = u_col0 // w
    ctx_blk0 = (b * s) // tc
    halo_per_chunk = tc // SUBLANES
    n_halo = r_rows // SUBLANES

    def fwd_blk(bi_, st):
        return jnp.where(st == 0, ctx_blk0 + bi_, bi_ * n_chunks + st - 1)

    def bwd_blk(bi_, st):
        return jnp.where(st == 0, ctx_blk0 + bi_, bi_ * n_chunks + n_chunks - st)

    def main(blk):
        return lambda bi_, st: (blk(bi_, st), ucol)

    def prev(blk):
        return lambda bi_, st: (jnp.maximum(blk(bi_, st) * halo_per_chunk - 1, 0),
                                ucol)

    def nxt(blk):
        return lambda bi_, st: (jnp.minimum((blk(bi_, st) + 1) * halo_per_chunk, n_halo - 1),
                                ucol)

    def out_map(blk):
        return lambda bi_, st: (blk(bi_, st), 0)

    def full(shape):
        return pl.BlockSpec(shape, lambda bi_, st: (0,) * len(shape))

    t_idx = jnp.arange(tc)
    src = (t_idx % SUBLANES) * sub + t_idx // SUBLANES
    perm = (src[:, None] == t_idx[None, :]).astype(BF16)
    permt = perm.T

    n_blk = w // LRU_BLOCK_DIM
    out = pl.pallas_call(
        functools.partial(_lru_kernel, n_chunks=n_chunks),
        out_shape=[jax.ShapeDtypeStruct((r_rows, w), BF16)] * 2,
        grid=(b, n_chunks + 1),
        in_specs=[full((tc, tc)), full((tc, tc)),
                  pl.BlockSpec((tc, w), main(fwd_blk)),
                  pl.BlockSpec((SUBLANES, w), prev(fwd_blk)),
                  pl.BlockSpec((SUBLANES, w), nxt(fwd_blk)),
                  pl.BlockSpec((tc, w), main(bwd_blk)),
                  pl.BlockSpec((SUBLANES, w), prev(bwd_blk)),
                  pl.BlockSpec((SUBLANES, w), nxt(bwd_blk)),
                  full((CONV_WIDTH, w)), full((1, w)),
                  full((2, n_blk, LRU_BLOCK_DIM, LRU_BLOCK_DIM)), full((2, w)),
                  full((2, n_blk, LRU_BLOCK_DIM, LRU_BLOCK_DIM)), full((2, w)),
                  full((2, w))],
        out_specs=[pl.BlockSpec((tc, w), out_map(fwd_blk)),
                   pl.BlockSpec((tc, w), out_map(bwd_blk))],
        scratch_shapes=[pltpu.VMEM((tc + 3 * SUBLANES, w), F32),
                        pltpu.VMEM((tc, w), F32),
                        pltpu.VMEM((tc, w), F32),
                        pltpu.VMEM((2 * SUBLANES, w), F32),
                        pltpu.VMEM((2, w), F32)],
        compiler_params=_params("arbitrary", "arbitrary"),
        name="conv_rglru",
    )(perm, permt, z_rest, z_rest, z_rest, z_rest, z_rest, z_rest,
      conv_w, conv_b.reshape(1, w), wr, br, wi, bi, lam)
    return out


def _gelu_tanh(x):
    return 0.5 * x * (1.0 + jnp.tanh(math.sqrt(2.0 / math.pi) * (x + 0.044715 * (x * x * x))))


def _rec_gate_kernel(hf_ref, hb_ref, gl_ref, o_ref):
    rec = hf_ref[...].astype(F32) + hb_ref[...].astype(F32)
    o_ref[...] = (rec * _gelu_tanh(gl_ref[...].astype(F32))).astype(o_ref.dtype)


def _rec_gate(hf, hb, z_rest, gl_col0, tm, n_rows):
    r, w = hf.shape
    assert gl_col0 % w == 0
    gl_blk = gl_col0 // w
    return pl.pallas_call(
        _rec_gate_kernel,
        out_shape=jax.ShapeDtypeStruct((r, w), BF16),
        grid=(n_rows // tm,),
        in_specs=[pl.BlockSpec((tm, w), lambda i: (i, 0)),
                  pl.BlockSpec((tm, w), lambda i: (i, 0)),
                  pl.BlockSpec((tm, w), lambda i: (i, gl_blk))],
        out_specs=pl.BlockSpec((tm, w), lambda i: (i, 0)),
        compiler_params=_params("arbitrary"),
        name="rec_gate",
    )(hf, hb, z_rest)


def _merge_kernel(att_ref, rec_ref, woa_ref, wob_ref, ga_ref, gb_ref, o_ref, woa_s, wob_s):
    @pl.when(pl.program_id(1) == 0)
    def _():
        woa_s[...] = woa_ref[...].astype(BF16)
        wob_s[...] = wob_ref[...].astype(BF16)

    ya = jnp.dot(att_ref[...], woa_s[...], preferred_element_type=F32)
    yb = jnp.dot(rec_ref[...], wob_s[...], preferred_element_type=F32)
    out = _sigmoid(ga_ref[...].astype(F32)) * ya + _sigmoid(gb_ref[...].astype(F32)) * yb
    o_ref[...] = out.astype(o_ref.dtype)


def _merge(att, rec, z_rest, cols, w_oa_stack, w_ob_stack, layer, n_rows, tn):
    r, qd = att.shape
    w = rec.shape[1]
    d = w_oa_stack.shape[2]
    tm = _row_tile(n_rows)
    assert cols["ga"] % tn == 0 and cols["gb"] % tn == 0
    ga_blk, gb_blk = cols["ga"] // tn, cols["gb"] // tn
    return pl.pallas_call(
        _merge_kernel,
        out_shape=jax.ShapeDtypeStruct((r, d), BF16),
        grid=(d // tn, n_rows // tm),
        in_specs=[pl.BlockSpec((tm, qd), lambda j, i: (i, 0)),
                  pl.BlockSpec((tm, w), lambda j, i: (i, 0)),
                  pl.BlockSpec((None, qd, tn), lambda j, i: (layer, 0, j)),
                  pl.BlockSpec((None, w, tn), lambda j, i: (layer, 0, j)),
                  pl.BlockSpec((tm, tn), lambda j, i: (i, ga_blk + j)),
                  pl.BlockSpec((tm, tn), lambda j, i: (i, gb_blk + j))],
        out_specs=pl.BlockSpec((tm, tn), lambda j, i: (i, j)),
        scratch_shapes=[pltpu.VMEM((qd, tn), BF16), pltpu.VMEM((w, tn), BF16)],
        compiler_params=_params("arbitrary", "arbitrary"),
        name="gated_merge",
    )(att, rec, w_oa_stack, w_ob_stack, z_rest, z_rest)


def _row_copy(src_ref, src_row, dst_ref, dst_row, sem):
    return pltpu.make_async_copy(src_ref.at[pl.ds(src_row, 1)], dst_ref.at[pl.ds(dst_row, 1)], sem)


def _dispatch_kernel(pos_ref, h_ref, xs_ref, sem):
    tm = h_ref.shape[0]

    def copy(r, k):
        return _row_copy(h_ref, r, xs_ref, pos_ref[0, r * TOP_K + k], sem)

    def start(r, carry):
        for k in range(TOP_K):
            copy(r, k).start()
        return carry

    def wait(r, carry):
        for k in range(TOP_K):
            copy(r, k).wait()
        return carry

    lax.fori_loop(0, tm, start, 0)
    lax.fori_loop(0, tm, wait, 0)


def _dispatch(hp, pos3, n_slots, tm, n_rows):
    r, dh = hp.shape
    return pl.pallas_call(
        _dispatch_kernel,
        out_shape=jax.ShapeDtypeStruct((n_slots, dh), jnp.uint32),
        grid=(n_rows // tm,),
        in_specs=[pl.BlockSpec((None, 1, tm * TOP_K), lambda i: (i, 0, 0), memory_space=pltpu.SMEM),
                  pl.BlockSpec((tm, dh), lambda i: (i, 0))],
        out_specs=pl.BlockSpec(memory_space=pl.ANY),
        scratch_shapes=[pltpu.SemaphoreType.DMA(())],
        compiler_params=_params("arbitrary"),
        name="moe_dispatch",
    )(pos3, hp)


def _expert_kernel(te_ref, ti_ref, tv_ref, xs_ref, w1_ref, b1_ref, w2_ref, ys_ref, w1_s, w2_s, *,
                   expert_dim):
    t = pl.program_id(0)
    new_expert = (t == 0) | (te_ref[t] != te_ref[jnp.maximum(t - 1, 0)])

    @pl.when(new_expert)
    def _():
        w1_s[...] = w1_ref[...].astype(BF16)
        w2_s[...] = w2_ref[...].astype(BF16)

    @pl.when(tv_ref[t] > 0)
    def _():
        p = xs_ref[...]
        row = lax.broadcasted_iota(jnp.int32, p.shape, 0)
        p = jnp.where(row < tv_ref[t], p, jnp.uint32(0))
        lo, hi = _unpack_halves(p)
        xb = jnp.concatenate([lo.astype(BF16), hi.astype(BF16)], axis=1)
        hdn = jnp.dot(xb, w1_s[...], preferred_element_type=F32) + b1_ref[...]
        glu = jnp.minimum(hdn[:, :expert_dim], SWIGLU_LIMIT)
        lin = jnp.clip(hdn[:, expert_dim:], -SWIGLU_LIMIT, SWIGLU_LIMIT)
        act = glu * _sigmoid(SWIGLU_ALPHA * glu) * (lin + 1.0)
        y = jnp.dot(act.astype(BF16), w2_s[...], preferred_element_type=F32)
        ys_ref[...] = _pack_halves(y)


def _experts(xs, tiles, w1_stack, b1_stack, w2_stack, layer, capacity, tm):
    n_slots, dh = xs.shape
    _, n_e, d, f2 = w1_stack.shape
    f = f2 // 2
    te, ti, tv = tiles
    cap_tiles = capacity // tm

    def rows(t, te_r, ti_r, tv_r):
        return (te_r[t] * cap_tiles + ti_r[t], 0)

    grid_spec = pltpu.PrefetchScalarGridSpec(
        num_scalar_prefetch=3,
        grid=(te.shape[0],),
        in_specs=[pl.BlockSpec((tm, dh), rows),
                  pl.BlockSpec((None, None, d, f2), lambda t, te_r, ti_r, tv_r: (layer, te_r[t], 0, 0)),
                  pl.BlockSpec((None, 1, f2), lambda t, te_r, ti_r, tv_r: (layer * n_e + te_r[t], 0, 0)),
                  pl.BlockSpec((None, None, f, d), lambda t, te_r, ti_r, tv_r: (layer, te_r[t], 0, 0))],
        out_specs=pl.BlockSpec((tm, dh), rows),
        scratch_shapes=[pltpu.VMEM((d, f2), BF16), pltpu.VMEM((f, d), BF16)],
    )
    return pl.pallas_call(
        functools.partial(_expert_kernel, expert_dim=f),
        out_shape=jax.ShapeDtypeStruct((n_slots, dh), jnp.uint32),
        grid_spec=grid_spec,
        compiler_params=_params("arbitrary"),
        name="moe_experts",
    )(te, ti, tv, xs, w1_stack, b1_stack.reshape(-1, 1, f2), w2_stack)


def _combine_norm_kernel(pos_ref, pos_next_ref, wts_ref, comb_ref, b2_ref, ys_ref, x_ref, gate_ref,
                         g_ref, *refs, final):
    if final:
        o_ref, buf_s, sem = refs
    else:
        sh_ref, sc_ref, xo_ref, h_ref, buf_s, sem = refs
    i = pl.program_id(0)
    n = pl.num_programs(0)
    tm = wts_ref.shape[0]
    half = x_ref.shape[1] // 2
    slot = lax.rem(i, 2)

    def gather(p_ref, s, do_start):
        def copy(r, k):
            return _row_copy(ys_ref, p_ref[0, r * TOP_K + k], buf_s.at[s, k], r, sem.at[s])

        def body(r, carry):
            for k in range(TOP_K):
                if do_start:
                    copy(r, k).start()
                else:
                    copy(r, k).wait()
            return carry

        lax.fori_loop(0, tm, body, 0)

    @pl.when(i == 0)
    def _():
        gather(pos_ref, slot, True)

    @pl.when(i + 1 < n)
    def _():
        gather(pos_next_ref, 1 - slot, True)

    bias = jnp.dot(comb_ref[...], b2_ref[...], precision=HIGHEST, preferred_element_type=F32)
    gather(pos_ref, slot, False)
    acc_lo = bias[:, :half]
    acc_hi = bias[:, half:]
    wts = wts_ref[...]
    for k in range(TOP_K):
        lo, hi = _unpack_halves(buf_s[slot, k])
        wk = wts[:, k:k + 1]
        acc_lo = acc_lo + wk * lo
        acc_hi = acc_hi + wk * hi
    x = x_ref[...] + gate_ref[...] * jnp.concatenate([acc_lo, acc_hi], axis=1)
    ms = jnp.mean(x * x, axis=-1, keepdims=True)
    y = (x * lax.rsqrt(ms + EPS)) * g_ref[...]
    if final:
        o_ref[...] = y
    else:
        xo_ref[...] = x
        h_ref[...] = (y * (1.0 + sc_ref[...]) + sh_ref[...]).astype(h_ref.dtype)


def _combine_norm(ys, pos3, wts, comb, b2_stack, layer, x, gate_mods, group_of_block, tm, n_rows, g,
                  mods=None):
    _, dh = ys.shape
    _, n_e, d = b2_stack.shape
    r = x.shape[0]
    final = mods is None
    n_tiles = n_rows // tm

    def row(i):
        return (i, 0)

    def mod_spec(m):
        return pl.BlockSpec((None, 1, d), lambda i: (group_of_block(i) * N_MOD + m, 0, 0))

    in_specs = [pl.BlockSpec((None, 1, tm * TOP_K), lambda i: (i, 0, 0), memory_space=pltpu.SMEM),
                pl.BlockSpec((None, 1, tm * TOP_K), lambda i: (jnp.minimum(i + 1, n_tiles - 1), 0, 0),
                             memory_space=pltpu.SMEM),
                pl.BlockSpec((tm, TOP_K), row),
                pl.BlockSpec((tm, n_e), row),
                pl.BlockSpec((None, n_e, d), lambda i: (layer, 0, 0)),
                pl.BlockSpec(memory_space=pl.ANY),
                pl.BlockSpec((tm, d), row),
                mod_spec(N_MOD - 1),
                pl.BlockSpec((1, d), lambda i: (0, 0))]
    args = [pos3, pos3, wts, comb, b2_stack, ys, x, gate_mods, g.reshape(1, d)]
    if final:
        out_shape = jax.ShapeDtypeStruct((n_rows, d), F32)
        out_specs = pl.BlockSpec((tm, d), row)
        aliases = {}
    else:
        in_specs += [mod_spec(0), mod_spec(1)]
        args += [mods, mods]
        out_shape = [jax.ShapeDtypeStruct((r, d), F32), jax.ShapeDtypeStruct((r, d), BF16)]
        out_specs = [pl.BlockSpec((tm, d), row), pl.BlockSpec((tm, d), row)]
        aliases = {6: 0}
    return pl.pallas_call(
        functools.partial(_combine_norm_kernel, final=final),
        out_shape=out_shape,
        grid=(n_tiles,),
        in_specs=in_specs,
        out_specs=out_specs,
        scratch_shapes=[pltpu.VMEM((2, TOP_K, tm, dh), jnp.uint32), pltpu.SemaphoreType.DMA((2,))],
        input_output_aliases=aliases,
        compiler_params=_params("arbitrary"),
        name="moe_combine_final" if final else "moe_combine_norm",
    )(*args)


def _expert_tiles(counts, tm, n_tiles):
    n_e = counts.shape[0]
    per = (counts + tm - 1) // tm
    ends = jnp.cumsum(per)
    total = ends[-1]
    t = jnp.minimum(jnp.arange(n_tiles, dtype=jnp.int32), total - 1)
    te = jnp.minimum(jnp.sum(t[:, None] >= ends[None, :], axis=1), n_e - 1).astype(jnp.int32)
    mine = jnp.arange(n_e, dtype=jnp.int32)[None, :] == te[:, None]
    ti = (t - jnp.sum(jnp.where(mine, (ends - per)[None, :], 0), axis=1)).astype(jnp.int32)
    tv = jnp.clip(jnp.sum(jnp.where(mine, counts[None, :], 0), axis=1) - ti * tm, 0, tm)
    tv = jnp.where(jnp.arange(n_tiles) < total, tv, 0).astype(jnp.int32)
    return te, ti, tv


def _rope_tables(b, s, c):
    rows = jnp.arange(s)
    pos_r = (rows // GRID_W).astype(F32)
    pos_c = (rows % GRID_W).astype(F32)
    axis_dim = HEAD_DIM // 2
    inv = ROPE_THETA ** (-jnp.arange(0, axis_dim, 2, dtype=F32) / axis_dim)
    ang_r = pos_r[:, None] * inv[None, :]
    ang_c = pos_c[:, None] * inv[None, :]
    ang = jnp.concatenate([ang_r, ang_r, ang_c, ang_c], axis=-1)
    sign = jnp.where((jnp.arange(HEAD_DIM) % (HEAD_DIM // 2)) < HEAD_DIM // 4, -1.0, 1.0)
    cos = jnp.concatenate([jnp.tile(jnp.cos(ang), (b, 1)), jnp.ones((b * c, HEAD_DIM), F32)], axis=0)
    sin = jnp.concatenate([jnp.tile(jnp.sin(ang) * sign, (b, 1)),
                           jnp.zeros((b * c, HEAD_DIM), F32)], axis=0)
    return cos, sin


def kernel(x, c, ctx, c_ctx, w_mod1, w_mod2, b_mod, g_mix, g_ffn, w_in, conv_w, conv_b, lru_wr,
           lru_br, lru_wi, lru_bi, lru_lam, sinks, w_oa, w_ob, w_out, w_router, b_router, w_exp1,
           b_exp1, w_exp2, b_exp2, g_final):
    b, s, d = x.shape
    cl = ctx.shape[1]
    depth = w_in.shape[0]
    lw = conv_w.shape[2]
    q_dim = N_Q_HEADS * HEAD_DIM
    kv_dim = N_KV_HEADS * HEAD_DIM
    n_lat = b * s
    n_all = n_lat + b * cl
    dims = {"B": b, "S": s, "C": cl}
    assert s % BLOCK == 0 and cl % BLOCK == 0 and n_lat % cl == 0

    tm = _tile(math.gcd(s, b * cl), ELEMENTWISE_ROW_TILE)
    tm_norm = _tile(tm, NORM_ROW_TILE)

    def group_of(tile):
        def f(i):
            return jnp.where(i * tile < n_lat, (i * tile) // s, b)
        return f

    g8 = -(-(b + 1) // SUBLANES) * SUBLANES
    cond = jnp.zeros((g8, d), F32).at[:b].set(c).at[b].set(c_ctx)
    xa = (x.reshape(n_lat, d), ctx.reshape(b * cl, d))
    cos, sin = _rope_tables(b, s, cl)

    tn = _tile(math.gcd(math.gcd(q_dim, kv_dim), math.gcd(lw, d)), WEIGHT_COL_TILE)
    o_v = q_dim + kv_dim
    o_u = o_v + kv_dim
    n_main = (2 * lw + 2 * d) // tn
    qk_blocks = (o_v // tn, lambda j: j)
    rest_blocks = (n_main + kv_dim // tn,
                   lambda j: jnp.where(j < n_main, o_u // tn + j, o_v // tn + j - n_main))
    out_blocks = (d // tn, lambda j: j)
    rest_cols = {"u": 0, "gl": lw, "ga": 2 * lw, "gb": 2 * lw + d, "v": 2 * lw + 2 * d}

    n_experts = w_router.shape[2]
    tm_moe = tm_norm
    capacity = -(-n_all // tm_moe) * tm_moe
    n_slots = n_experts * capacity

    moe = None
    for l in range(depth):
        last = l == depth - 1
        n_rows = n_lat if last else n_all
        mods = _modulation(cond, w_mod1, w_mod2, b_mod, l).reshape(g8 * N_MOD, 1, d)

        if moe is None:
            xa, h = _resnorm(xa, mods, group_of(tm_norm), tm_norm, n_all, g_mix[l], 0, 1)
        else:
            xa, h = _combine_norm(*moe, l - 1, xa, mods_prev, group_of(tm_moe), tm_moe, n_all,
                                  g_mix[l], mods=mods)

        z_qk = _matmul(h, w_in, l, qk_blocks, n_all, tn, rope=(cos, sin))
        z_rest = _matmul(h, w_in, l, rest_blocks, n_all, tn)

        att = _attention(z_qk, z_rest, sinks[l], None, dims, rest_cols["v"], context_queries=False)
        if not last:
            att = _attention(z_qk, z_rest, sinks[l], att, dims, rest_cols["v"], context_queries=True)
        hf, hb = _lru(z_rest, rest_cols["u"], conv_w[l], conv_b[l], lru_wr[l].astype(BF16), lru_br[l],
                      lru_wi[l].astype(BF16), lru_bi[l], lru_lam[l], dims)
        rec = _rec_gate(hf, hb, z_rest, rest_cols["gl"], tm, n_rows)
        merged = _merge(att, rec, z_rest, rest_cols, w_oa, w_ob, l, n_rows, tn)
        y = _matmul(merged, w_out, l, out_blocks, n_rows, tn)
        xa, hp, comb, pos, wts, counts = _resnorm(
            xa, mods, group_of(tm_norm), tm_norm, n_rows, g_ffn[l], 3, 4, delta=y, gate_mods=mods,
            gate_idx=2, router=(w_router[l], b_router[l]), capacity=capacity)
        pos3 = pos.reshape(n_all // tm_moe, 1, tm_moe * TOP_K)
        tiles = _expert_tiles(counts[0], tm_moe, n_rows * TOP_K // tm_moe + n_experts)
        xs = _dispatch(hp, pos3, n_slots, tm_moe, n_rows)
        ys = _experts(xs, tiles, w_exp1, b_exp1, w_exp2, l, capacity, tm_moe)
        moe = (ys, pos3, wts, comb, b_exp2)
        mods_prev = mods

    out = _combine_norm(*moe, depth - 1, xa, mods_prev, group_of(tm_moe), tm_moe, n_lat, g_final)
    return out.reshape(b, s, d)
```
